```python
import math
import jax, jax.numpy as jnp
from jax import lax
import numpy as np

D_MODEL = 2048
BATCH = 16
SEQ = 256
DEPTH = 4
DEC_BATCH = 2
DEC_SEQ = 1024
PAST_LEN = 256

GRID_W = 64
N_EVEN = (DEPTH + 1) // 2
N_ODD = DEPTH // 2
N_MOD = 9
FFN_HIDDEN = 5632
EPS = 1e-6
NEG = -1e30

S5_WIDTH = D_MODEL // 2
S5_GROUP = 16
S5_GROUPS = S5_WIDTH // S5_GROUP
S5_P = 64
RET_HEADS = 8
RET_DK = (D_MODEL // 2) // RET_HEADS
RET_DV = RET_DK
RET_WIDTH = RET_HEADS * RET_DV
RET_CHUNK = 128
EVEN_IN = S5_WIDTH + 4 * RET_WIDTH

POOL_WIDTH = D_MODEL // 2
POOL_WINDOWS = (2, 4, 8, 16)
POOL_GROUPS = 4
POOL_GC = POOL_WIDTH // POOL_GROUPS
ATT_HEADS = 8
ATT_KV = 2
ATT_HD = (D_MODEL // 2) // ATT_HEADS
ATT_GROUP = ATT_HEADS // ATT_KV
ATT_WIN = 128
ATT_BLOCK = 128
ODD_IN = POOL_WIDTH + (ATT_HEADS + 2 * ATT_KV) * ATT_HD
ROPE_BASE = 10000.0

kernel_name = 'hybrid_diffusion_prefix_trunk_step'

F32 = jnp.float32


def rmsnorm(x, g):
    x32 = x.astype(F32)
    y = x32 * lax.rsqrt(jnp.mean(x32 * x32, axis=-1, keepdims=True) + EPS)
    return (y * g.astype(F32)).astype(x.dtype)


def adaln_in(x, mod, j, g):
    h = rmsnorm(x, g).astype(F32)
    return (h * (1.0 + mod[:, 3 * j + 1, None]) + mod[:, 3 * j, None]).astype(x.dtype)


def adaln_out(x, y, mod, j, g, weight):
    return (x.astype(F32) + weight * mod[:, 3 * j + 2, None] * rmsnorm(y, g).astype(F32)).astype(x.dtype)


def swiglu(h, w_gate, w_up, w_down):
    return (jax.nn.silu(h @ w_gate) * (h @ w_up)) @ w_down


def s5_scan(u, lam_re, lam_im, log_dt, b_re, b_im, c_re, c_im, h0):
    lam = lax.complex(jnp.minimum(lam_re.astype(F32), -1e-4), lam_im.astype(F32))
    dt = jnp.exp(log_dt.astype(F32))[:, None]
    a_bar = jnp.exp(lam * dt)
    b = lax.complex(b_re.astype(F32), b_im.astype(F32))
    b_bar = ((a_bar - 1.0) / lam)[..., None] * b
    c = lax.complex(c_re.astype(F32), c_im.astype(F32))
    bu = jnp.einsum('nlgc,gpc->nlgp', u.astype(jnp.complex64), b_bar)
    bu = bu.at[:, 0].add(a_bar * h0)
    a = jnp.broadcast_to(a_bar, bu.shape)

    def combine(e1, e2):
        a1, b1 = e1
        a2, b2 = e2
        return a1 * a2, a2 * b1 + b2

    _, h = lax.associative_scan(combine, (a, bu), axis=1)
    y = jnp.real(jnp.einsum('nlgp,gcp->nlgc', h, c))
    return y, h[:, -1]


def s5_mixer(u, lam_re, lam_im, log_dt, b_re, b_im, c_re, c_im, d_skip, w_glu, b_glu, h0_re, h0_im):
    n, L, _ = u.shape
    u32 = u.astype(F32)
    ug = u32.reshape(n, L, S5_GROUPS, S5_GROUP)
    h0 = lax.complex(h0_re.astype(F32), h0_im.astype(F32))
    y_f, s_f = s5_scan(ug, lam_re[0], lam_im[0], log_dt[0], b_re[0], b_im[0], c_re[0], c_im[0], h0[:, 0])
    y_b, s_b = s5_scan(jnp.flip(ug, 1), lam_re[1], lam_im[1], log_dt[1], b_re[1], b_im[1], c_re[1], c_im[1], h0[:, 1])
    y = (y_f + jnp.flip(y_b, 1)).reshape(n, L, S5_WIDTH) + d_skip.astype(F32) * u32
    z = jax.nn.gelu(y)
    out = z * jax.nn.sigmoid(z @ w_glu.astype(F32) + b_glu.astype(F32))
    s_fin = jnp.stack([s_f, s_b], axis=1)
    return out.astype(u.dtype), jnp.real(s_fin), jnp.imag(s_fin)


def retention_scan(q, k, v, log_gamma, s0):
    n, L, H, dk = q.shape
    nc = L // RET_CHUNK
    qc = q.reshape(n, nc, RET_CHUNK, H, dk)
    kc = k.reshape(n, nc, RET_CHUNK, H, dk)
    vc = v.reshape(n, nc, RET_CHUNK, H, v.shape[-1])
    pos = jnp.arange(RET_CHUNK, dtype=F32)
    lg = log_gamma.astype(F32)[:, None]
    rel = pos[:, None] - pos[None, :]
    decay_mask = jnp.where(rel >= 0, jnp.exp(jnp.maximum(rel, 0.0)[None] * lg[:, :, None]), 0.0)
    inner = jnp.einsum('ncihd,ncjhd->nchij', qc, kc) * decay_mask
    o_inner = jnp.einsum('nchij,ncjhe->ncihe', inner, vc)
    k_dec = kc * jnp.exp((RET_CHUNK - 1.0 - pos)[None, :] * lg).T[None, None, :, :, None]
    kv = jnp.einsum('ncjhd,ncjhe->nchde', k_dec, vc)
    chunk_decay = jnp.exp(RET_CHUNK * lg)[..., None]

    def step(s, kv_c):
        return chunk_decay * s + kv_c, s

    s_final, s_start = lax.scan(step, s0, jnp.moveaxis(kv, 1, 0))
    s_start = jnp.moveaxis(s_start, 0, 1)
    q_dec = qc * jnp.exp((pos + 1.0)[None, :] * lg).T[None, None, :, :, None]
    o_cross = jnp.einsum('ncihd,nchde->ncihe', q_dec, s_start)
    return (o_inner + o_cross).reshape(n, L, H, -1), s_final


def retention_mixer(q, k, v, g, decay_logit, gn_g, s0):
    n, L, _ = q.shape
    q = q.astype(F32).reshape(n, L, RET_HEADS, RET_DK)
    k = k.astype(F32).reshape(n, L, RET_HEADS, RET_DK) * (RET_DK ** -0.5)
    v = v.astype(F32).reshape(n, L, RET_HEADS, RET_DV)
    lg = jax.nn.log_sigmoid(decay_logit.astype(F32))
    s0 = s0.astype(F32)
    o_f, s_f = retention_scan(q, k, v, lg[0], s0[:, 0])
    o_b, s_b = retention_scan(jnp.flip(q, 1), jnp.flip(k, 1), jnp.flip(v, 1), lg[1], s0[:, 1])
    o = o_f + jnp.flip(o_b, 1)
    mu = jnp.mean(o, axis=-1, keepdims=True)
    var = jnp.mean(jnp.square(o - mu), axis=-1, keepdims=True)
    o = ((o - mu) * lax.rsqrt(var + EPS)).reshape(n, L, RET_WIDTH) * gn_g.astype(F32)
    out = jax.nn.silu(g.astype(F32)) * o
    return out.astype(g.dtype), jnp.stack([s_f, s_b], axis=1)


def pool_mixer(u, w_pool, pool_scale):
    n, L, _ = u.shape
    u32 = u.astype(F32).reshape(n, L, POOL_GROUPS, POOL_GC)
    cs = jnp.concatenate([jnp.zeros_like(u32[:, :1]), jnp.cumsum(u32, axis=1)], axis=1)
    t = jnp.arange(L)
    outs = []
    for gi, w in enumerate(POOL_WINDOWS):
        lo = jnp.clip(t - w // 2, 0, L - 1)
        hi = jnp.clip(t + w // 2 - 1, 0, L - 1)
        cnt = (hi - lo + 1).astype(F32)
        mean = (cs[:, hi + 1, gi] - cs[:, lo, gi]) / cnt[None, :, None]
        outs.append(mean - u32[:, :, gi])
    pooled = jnp.stack(outs, axis=2)
    mixed = jnp.einsum('nlgc,gcd->nlgd', pooled, w_pool.astype(F32))
    return (mixed.reshape(n, L, POOL_WIDTH) * pool_scale.astype(F32)).astype(u.dtype)


def rope_axis(x, pos, n_freq):
    freqs = ROPE_BASE ** (-jnp.arange(n_freq, dtype=F32) / n_freq)
    ang = pos.astype(F32)[:, None] * freqs[None]
    cos = jnp.cos(ang)[None, :, None, :]
    sin = jnp.sin(ang)[None, :, None, :]
    x1, x2 = x[..., :n_freq], x[..., n_freq:]
    return jnp.concatenate([x1 * cos - x2 * sin, x1 * sin + x2 * cos], axis=-1)


def axial_rope(x):
    L = x.shape[1]
    n_rows = L // GRID_W
    rows = jnp.repeat(jnp.arange(n_rows), GRID_W)
    cols = jnp.tile(jnp.arange(GRID_W), n_rows)
    half = x.shape[-1] // 2
    return jnp.concatenate([rope_axis(x[..., :half], rows, half // 2),
                            rope_axis(x[..., half:], cols, half // 2)], axis=-1)


def attend_block(qb, k, v, mask, sink):
    s = jnp.einsum('nqkgd,nskd->nkgqs', qb, k) * (ATT_HD ** -0.5)
    if mask is not None:
        s = jnp.where(mask, s, NEG)
    sk = sink[None, :, :, None, None]
    m = jnp.maximum(jnp.max(s, axis=-1, keepdims=True), sk)
    p = jnp.exp(s - m)
    denom = jnp.sum(p, axis=-1, keepdims=True) + jnp.exp(sk - m)
    return jnp.einsum('nkgqs,nskd->nqkgd', p / denom, v)


def ctx_attention(q, k, v, sink):
    n, L = q.shape[:2]
    nb = L // ATT_BLOCK
    qb = jnp.moveaxis(q.reshape(n, nb, ATT_BLOCK, ATT_KV, ATT_GROUP, ATT_HD), 1, 0)
    o = lax.map(lambda qi: attend_block(qi, k, v, None, sink), qb)
    return jnp.moveaxis(o, 0, 1).reshape(n, L, ATT_HEADS * ATT_HD)


def latent_attention(q, k, v, k_ctx, v_ctx, sink):
    n, L = q.shape[:2]
    nb = L // ATT_BLOCK
    span = 3 * ATT_BLOCK
    qb = jnp.moveaxis(q.reshape(n, nb, ATT_BLOCK, ATT_KV, ATT_GROUP, ATT_HD), 1, 0)
    pad = ((0, 0), (ATT_BLOCK, ATT_BLOCK), (0, 0), (0, 0))
    kp = jnp.pad(k, pad)
    vp = jnp.pad(v, pad)
    q_off = jnp.arange(ATT_BLOCK)
    k_off = jnp.arange(span) - ATT_BLOCK
    band = jnp.abs(k_off[None, :] - q_off[:, None]) <= ATT_WIN
    ctx_mask = jnp.ones((ATT_BLOCK, k_ctx.shape[1]), dtype=bool)

    def block(args):
        i, qi = args
        start = i * ATT_BLOCK
        kb = lax.dynamic_slice_in_dim(kp, start, span, axis=1)
        vb = lax.dynamic_slice_in_dim(vp, start, span, axis=1)
        kpos = start + k_off
        mask = band & ((kpos >= 0) & (kpos < L))[None, :]
        keys = jnp.concatenate([kb, k_ctx], axis=1)
        vals = jnp.concatenate([vb, v_ctx], axis=1)
        return attend_block(qi, keys, vals, jnp.concatenate([mask, ctx_mask], axis=1), sink)

    o = lax.map(block, (jnp.arange(nb), qb))
    return jnp.moveaxis(o, 0, 1).reshape(n, L, ATT_HEADS * ATT_HD)


def trunk(x, cond, is_ctx, s5_re, s5_im, ret_state, ck, cv, W):
    (w_mod, b_mod, norm_g, ffn1_gate, ffn1_up, ffn1_down, ffn2_gate, ffn2_up, ffn2_down,
     even_w_in, even_w_out, s5_lam_re, s5_lam_im, s5_log_dt, s5_b_re, s5_b_im, s5_c_re, s5_c_im,
     s5_d, s5_glu_w, s5_glu_b, ret_decay_logit, ret_gn_g,
     odd_w_in, odd_w_out, pool_w, pool_scale, att_sink) = W
    n, L, _ = x.shape
    dt = x.dtype
    silu_c = jax.nn.silu(cond.astype(F32))
    out_re, out_im, out_ret, out_k, out_v = [], [], [], [], []
    for l in range(DEPTH):
        mod = (silu_c @ w_mod[l].astype(F32) + b_mod[l].astype(F32)).reshape(-1, N_MOD, D_MODEL)
        h = adaln_in(x, mod, 0, norm_g[l, 0])
        x = adaln_out(x, swiglu(h, ffn1_gate[l], ffn1_up[l], ffn1_down[l]), mod, 0, norm_g[l, 1], 0.5)
        h = adaln_in(x, mod, 1, norm_g[l, 2])
        if l % 2 == 0:
            e = l // 2
            proj = h @ even_w_in[e]
            u, q, k, v, g = jnp.split(proj, [S5_WIDTH, S5_WIDTH + RET_WIDTH, S5_WIDTH + 2 * RET_WIDTH,
                                             S5_WIDTH + 3 * RET_WIDTH], axis=-1)
            if is_ctx:
                h0_re = jnp.zeros((n, 2, S5_GROUPS, S5_P), F32)
                h0_im = jnp.zeros((n, 2, S5_GROUPS, S5_P), F32)
                r0 = jnp.zeros((n, 2, RET_HEADS, RET_DK, RET_DV), F32)
            else:
                h0_re, h0_im, r0 = s5_re[:, e], s5_im[:, e], ret_state[:, e]
            y_a, sr, si = s5_mixer(u, s5_lam_re[e], s5_lam_im[e], s5_log_dt[e], s5_b_re[e], s5_b_im[e],
                                   s5_c_re[e], s5_c_im[e], s5_d[e], s5_glu_w[e], s5_glu_b[e], h0_re, h0_im)
            y_b, rs = retention_mixer(q, k, v, g, ret_decay_logit[e], ret_gn_g[e], r0)
            y = jnp.concatenate([y_a, y_b], axis=-1) @ even_w_out[e]
            if is_ctx:
                out_re.append(sr)
                out_im.append(si)
                out_ret.append(rs)
        else:
            o = l // 2
            proj = h @ odd_w_in[o]
            u, q, k, v = jnp.split(proj, [POOL_WIDTH, POOL_WIDTH + ATT_HEADS * ATT_HD,
                                          POOL_WIDTH + (ATT_HEADS + ATT_KV) * ATT_HD], axis=-1)
            y_c = pool_mixer(u, pool_w[o], pool_scale[o])
            q = q.astype(F32).reshape(n, L, ATT_HEADS, ATT_HD)
            k = k.astype(F32).reshape(n, L, ATT_KV, ATT_HD)
            v = v.astype(F32).reshape(n, L, ATT_KV, ATT_HD)
            sink = att_sink[o].astype(F32).reshape(ATT_KV, ATT_GROUP)
            if is_ctx:
                y_d = ctx_attention(q, k, v, sink)
                out_k.append(k)
                out_v.append(v)
            else:
                y_d = latent_attention(axial_rope(q), axial_rope(k), v,
                                       ck[:, o].astype(F32), cv[:, o].astype(F32), sink)
            y = jnp.concatenate([y_c, y_d.astype(dt)], axis=-1) @ odd_w_out[o]
        x = adaln_out(x, y, mod, 1, norm_g[l, 3], 1.0)
        h = adaln_in(x, mod, 2, norm_g[l, 4])
        x = adaln_out(x, swiglu(h, ffn2_gate[l], ffn2_up[l], ffn2_down[l]), mod, 2, norm_g[l, 5], 0.5)
    if is_ctx:
        return (x, jnp.stack(out_re, axis=1), jnp.stack(out_im, axis=1), jnp.stack(out_ret, axis=1),
                jnp.stack(out_k, axis=1), jnp.stack(out_v, axis=1))
    return x


def setup_inputs(seed: int = 0) -> dict:
    key = jax.random.key(seed)
    ks = iter(jax.random.split(key, 48))

    def nrm(shape, scale):
        return jax.random.normal(next(ks), shape, F32) * scale

    D, F = D_MODEL, FFN_HIDDEN
    n_idx = jnp.arange(S5_P, dtype=F32)
    h_idx = jnp.arange(RET_HEADS, dtype=F32)
    ret_logit0 = jnp.log(2.0 ** (5.0 + h_idx) - 1.0)
    return {
        'x_prompt': nrm((BATCH, SEQ, D), 1.0),
        'x_sample': nrm((DEC_BATCH, DEC_SEQ, D), 1.0),
        'c': nrm((DEC_BATCH, D), 1.0),
        'state_s5_re': nrm((DEC_BATCH, N_EVEN, 2, S5_GROUPS, S5_P), 0.5),
        'state_s5_im': nrm((DEC_BATCH, N_EVEN, 2, S5_GROUPS, S5_P), 0.5),
        'state_ret': nrm((DEC_BATCH, N_EVEN, 2, RET_HEADS, RET_DK, RET_DV), 0.1),
        'cache_k': nrm((DEC_BATCH, N_ODD, PAST_LEN, ATT_KV, ATT_HD), 1.0),
        'cache_v': nrm((DEC_BATCH, N_ODD, PAST_LEN, ATT_KV, ATT_HD), 1.0),
        'c_ctx': nrm((D,), 1.0),
        'w_mod': nrm((DEPTH, D, N_MOD * D), 0.5 * D ** -0.5),
        'b_mod': nrm((DEPTH, N_MOD * D), 0.01),
        'norm_g': 1.0 + nrm((DEPTH, 6, D), 0.01),
        'ffn1_gate': nrm((DEPTH, D, F), D ** -0.5),
        'ffn1_up': nrm((DEPTH, D, F), D ** -0.5),
        'ffn1_down': nrm((DEPTH, F, D), F ** -0.5),
        'ffn2_gate': nrm((DEPTH, D, F), D ** -0.5),
        'ffn2_up': nrm((DEPTH, D, F), D ** -0.5),
        'ffn2_down': nrm((DEPTH, F, D), F ** -0.5),
        'even_w_in': nrm((N_EVEN, D, EVEN_IN), D ** -0.5),
        'even_w_out': nrm((N_EVEN, S5_WIDTH + RET_WIDTH, D), (S5_WIDTH + RET_WIDTH) ** -0.5),
        's5_lam_re': -0.5 + nrm((N_EVEN, 2, S5_GROUPS, S5_P), 0.01),
        's5_lam_im': math.pi * n_idx + nrm((N_EVEN, 2, S5_GROUPS, S5_P), 0.01),
        's5_log_dt': jnp.log(jax.random.uniform(next(ks), (N_EVEN, 2, S5_GROUPS), F32, 1e-3, 1e-1)),
        's5_b_re': nrm((N_EVEN, 2, S5_GROUPS, S5_P, S5_GROUP), (2 * S5_GROUP) ** -0.5),
        's5_b_im': nrm((N_EVEN, 2, S5_GROUPS, S5_P, S5_GROUP), (2 * S5_GROUP) ** -0.5),
        's5_c_re': nrm((N_EVEN, 2, S5_GROUPS, S5_GROUP, S5_P), (2 * S5_P) ** -0.5),
        's5_c_im': nrm((N_EVEN, 2, S5_GROUPS, S5_GROUP, S5_P), (2 * S5_P) ** -0.5),
        's5_d': nrm((N_EVEN, S5_WIDTH), 1.0),
        's5_glu_w': nrm((N_EVEN, S5_WIDTH, S5_WIDTH), S5_WIDTH ** -0.5),
        's5_glu_b': nrm((N_EVEN, S5_WIDTH), 0.01),
        'ret_decay_logit': ret_logit0 + nrm((N_EVEN, 2, RET_HEADS), 0.01),
        'ret_gn_g': 1.0 + nrm((N_EVEN, RET_WIDTH), 0.01),
        'odd_w_in': nrm((N_ODD, D, ODD_IN), D ** -0.5),
        'odd_w_out': nrm((N_ODD, POOL_WIDTH + ATT_HEADS * ATT_HD, D), (POOL_WIDTH + ATT_HEADS * ATT_HD) ** -0.5),
        'pool_w': nrm((N_ODD, POOL_GROUPS, POOL_GC, POOL_GC), POOL_GC ** -0.5),
        'pool_scale': 1.0 + nrm((N_ODD, POOL_WIDTH), 0.1),
        'att_sink': nrm((N_ODD, ATT_HEADS), 0.5),
    }


def reference(x_prompt, x_sample, c, state_s5_re, state_s5_im, state_ret, cache_k, cache_v, c_ctx,
              w_mod, b_mod, norm_g, ffn1_gate, ffn1_up, ffn1_down, ffn2_gate, ffn2_up, ffn2_down,
              even_w_in, even_w_out, s5_lam_re, s5_lam_im, s5_log_dt, s5_b_re, s5_b_im, s5_c_re, s5_c_im,
              s5_d, s5_glu_w, s5_glu_b, ret_decay_logit, ret_gn_g,
              odd_w_in, odd_w_out, pool_w, pool_scale, att_sink):
    W = (w_mod, b_mod, norm_g, ffn1_gate, ffn1_up, ffn1_down, ffn2_gate, ffn2_up, ffn2_down,
         even_w_in, even_w_out, s5_lam_re, s5_lam_im, s5_log_dt, s5_b_re, s5_b_im, s5_c_re, s5_c_im,
         s5_d, s5_glu_w, s5_glu_b, ret_decay_logit, ret_gn_g,
         odd_w_in, odd_w_out, pool_w, pool_scale, att_sink)
    y_prompt, new_s5_re, new_s5_im, new_ret, new_k, new_v = trunk(
        x_prompt, c_ctx[None, :], True, None, None, None, None, None, W)
    y_sample = trunk(x_sample, c, False, state_s5_re, state_s5_im, state_ret, cache_k, cache_v, W)
    return (y_prompt, y_sample, new_s5_re, new_s5_im, new_ret, new_k, new_v)
```

```python
import functools
import math

import jax
import jax.numpy as jnp
from jax import lax
from jax.experimental import pallas as pl
from jax.experimental.pallas import tpu as pltpu

F32 = jnp.float32
BF16 = jnp.bfloat16

D_MODEL = 2048
BATCH = 16
SEQ = 256
DEPTH = 4
DEC_BATCH = 2
DEC_SEQ = 1024
PAST_LEN = 256
GRID_W = 64
N_EVEN = (DEPTH + 1) // 2
N_ODD = DEPTH // 2
N_MOD = 9
FFN_HIDDEN = 5632
EPS = 1e-6
NEG = -1e30

S5_WIDTH = D_MODEL // 2
S5_GROUP = 16
S5_GROUPS = S5_WIDTH // S5_GROUP
S5_P = 64
RET_HEADS = 8
RET_DK = (D_MODEL // 2) // RET_HEADS
RET_DV = RET_DK
RET_WIDTH = RET_HEADS * RET_DV
RET_CHUNK = 128
EVEN_IN = S5_WIDTH + 4 * RET_WIDTH

POOL_WIDTH = D_MODEL // 2
POOL_WINDOWS = (2, 4, 8, 16)
POOL_GROUPS = 4
POOL_GC = POOL_WIDTH // POOL_GROUPS
ATT_HEADS = 8
ATT_KV = 2
ATT_HD = (D_MODEL // 2) // ATT_HEADS
ATT_GROUP = ATT_HEADS // ATT_KV
ATT_WIN = 128
ATT_BLOCK = 128
ODD_IN = POOL_WIDTH + (ATT_HEADS + 2 * ATT_KV) * ATT_HD
ROPE_BASE = 10000.0

N_CTX_TOK = BATCH * SEQ
N_DEC_TOK = DEC_BATCH * DEC_SEQ
N_TOK = N_CTX_TOK + N_DEC_TOK
MOD_ROWS = 8

S5_T = 16
S5_LANES = S5_T * S5_GROUP
S5_CTX_CHUNKS = SEQ // S5_T
S5_DEC_CHUNKS = DEC_SEQ // S5_T
S5_CTX_ROWS = BATCH * S5_CTX_CHUNKS
S5_DEC_ROWS = DEC_BATCH * S5_DEC_CHUNKS
S5_ROWS = S5_CTX_ROWS + S5_DEC_ROWS
S5_NPOW = 6

VMEM_LIMIT = 60 * 1024 * 1024


def _cparams(sem):
    return pltpu.CompilerParams(dimension_semantics=sem, vmem_limit_bytes=VMEM_LIMIT)


def _dot(a, b):
    return jnp.dot(a, b, preferred_element_type=F32)


def _rms(x, g):
    return (x * lax.rsqrt(jnp.mean(x * x, axis=-1, keepdims=True) + EPS)) * g


def _mod_row(i, tm):
    start = i * tm
    return (start >= N_CTX_TOK).astype(jnp.int32) + (start >= N_CTX_TOK + DEC_SEQ).astype(jnp.int32)


def _mod_kernel(c_ref, w_ref, b_ref, o_ref):
    c = c_ref[...]
    s = (c * jax.nn.sigmoid(c)).astype(BF16)
    o_ref[...] = _dot(s, w_ref[...].astype(BF16)) + b_ref[...]


def _modulation(c8, w_mod, b_mod):
    tn = 1024
    n = N_MOD * D_MODEL
    return pl.pallas_call(
        _mod_kernel,
        grid=(DEPTH, n // tn),
        in_specs=[
            pl.BlockSpec((MOD_ROWS, D_MODEL), lambda l, j: (0, 0)),
            pl.BlockSpec((None, D_MODEL, tn), lambda l, j: (l, 0, j)),
            pl.BlockSpec((None, 1, tn), lambda l, j: (l, 0, j)),
        ],
        out_specs=pl.BlockSpec((None, MOD_ROWS, tn), lambda l, j: (l, 0, j)),
        out_shape=jax.ShapeDtypeStruct((DEPTH, MOD_ROWS, n), F32),
        compiler_params=_cparams(("arbitrary", "arbitrary")),
        name="modulation",
    )(c8, w_mod, b_mod.reshape(DEPTH, 1, n))


def _mod_spec(l, j, tm):
    return pl.BlockSpec((None, None, None, 1, D_MODEL),
                        lambda i, *_: (l, _mod_row(i, tm), j, 0, 0))


def _gain_spec(l, j):
    return pl.BlockSpec((None, None, 1, D_MODEL), lambda i, *_: (l, j, 0, 0))


FFN_TM = 1024
FFN_TF = 256
ROW_CHUNK = 64


def _adaln_in_rows(x_ref, g_ref, sc_ref, sh_ref, h_ref, tm):
    def body(r, _):
        rows = pl.ds(pl.multiple_of(r * ROW_CHUNK, ROW_CHUNK), ROW_CHUNK)
        h = _rms(x_ref[rows, :], g_ref[...])
        h_ref[rows, :] = (h * (1.0 + sc_ref[...]) + sh_ref[...]).astype(BF16)
        return 0
    lax.fori_loop(0, tm // ROW_CHUNK, body, 0)


def _adaln_out_rows(x_ref, y_ref, g_ref, gate_ref, o_ref, weight, tm):
    def body(r, _):
        rows = pl.ds(pl.multiple_of(r * ROW_CHUNK, ROW_CHUNK), ROW_CHUNK)
        o_ref[rows, :] = x_ref[rows, :] + (weight * gate_ref[...]) * _rms(y_ref[rows, :], g_ref[...])
        return 0
    lax.fori_loop(0, tm // ROW_CHUNK, body, 0)


def _ffn_kernel(x_ref, sh_ref, sc_ref, gate_ref, gin_ref, gout_ref, wg_ref, wu_ref, wd_ref,
                o_ref, h_ref, *, nf, tm):
    f = pl.program_id(1)

    @pl.when(f == 0)
    def _():
        _adaln_in_rows(x_ref, gin_ref, sc_ref, sh_ref, h_ref, tm)

    h = h_ref[...]
    g = _dot(h, wg_ref[...].astype(BF16))
    u = _dot(h, wu_ref[...].astype(BF16))
    a = ((g * jax.nn.sigmoid(g)) * u).astype(BF16)
    contrib = _dot(a, wd_ref[...].astype(BF16))

    @pl.when(f == 0)
    def _():
        o_ref[...] = contrib

    @pl.when(f > 0)
    def _():
        o_ref[...] += contrib

    @pl.when(f == nf - 1)
    def _():
        _adaln_out_rows(x_ref, o_ref, gout_ref, gate_ref, o_ref, 0.5, tm)


def _ffn(x, mod5, norm4, w_gate, w_up, w_down, l, j, g_in, g_out):
    tm, tf = FFN_TM, FFN_TF
    nf = FFN_HIDDEN // tf
    return pl.pallas_call(
        functools.partial(_ffn_kernel, nf=nf, tm=tm),
        grid=(N_TOK // tm, nf),
        in_specs=[
            pl.BlockSpec((tm, D_MODEL), lambda i, f: (i, 0)),
            _mod_spec(l, 3 * j, tm), _mod_spec(l, 3 * j + 1, tm), _mod_spec(l, 3 * j + 2, tm),
            _gain_spec(l, g_in), _gain_spec(l, g_out),
            pl.BlockSpec((None, D_MODEL, tf), lambda i, f: (l, 0, f)),
            pl.BlockSpec((None, D_MODEL, tf), lambda i, f: (l, 0, f)),
            pl.BlockSpec((None, tf, D_MODEL), lambda i, f: (l, f, 0)),
        ],
        out_specs=pl.BlockSpec((tm, D_MODEL), lambda i, f: (i, 0)),
        out_shape=jax.ShapeDtypeStruct((N_TOK, D_MODEL), F32),
        scratch_shapes=[pltpu.VMEM((tm, D_MODEL), BF16)],
        compiler_params=_cparams(("parallel", "arbitrary")),
        name="ffn",
    )(x, mod5, mod5, mod5, norm4, norm4, w_gate, w_up, w_down)


PROJ_TM = 1024
PROJ_TN = 512


def _proj_in_kernel(x_ref, sh_ref, sc_ref, gin_ref, w_ref, o_ref, h_ref, *, tm):
    @pl.when(pl.program_id(1) == 0)
    def _():
        _adaln_in_rows(x_ref, gin_ref, sc_ref, sh_ref, h_ref, tm)

    o_ref[...] = _dot(h_ref[...], w_ref[...])


def _proj_in(x, mod5, norm4, w_in, l):
    tm, tn = PROJ_TM, PROJ_TN
    n = w_in.shape[-1]
    return pl.pallas_call(
        functools.partial(_proj_in_kernel, tm=tm),
        grid=(N_TOK // tm, n // tn),
        in_specs=[
            pl.BlockSpec((tm, D_MODEL), lambda i, k: (i, 0)),
            _mod_spec(l, 3, tm), _mod_spec(l, 4, tm),
            _gain_spec(l, 2),
            pl.BlockSpec((D_MODEL, tn), lambda i, k: (0, k)),
        ],
        out_specs=pl.BlockSpec((tm, tn), lambda i, k: (i, k)),
        out_shape=jax.ShapeDtypeStruct((N_TOK, n), F32),
        scratch_shapes=[pltpu.VMEM((tm, D_MODEL), BF16)],
        compiler_params=_cparams(("parallel", "arbitrary")),
        name="proj_in",
    )(x, mod5, mod5, norm4, w_in)


OUT_TM = 512


def _proj_out_kernel(x_ref, ya_ref, yb_ref, gate_ref, gout_ref, wa_ref, wb_ref, o_ref, y_ref, *, tm):
    y_ref[...] = _dot(ya_ref[...], wa_ref[...]) + _dot(yb_ref[...], wb_ref[...])
    _adaln_out_rows(x_ref, y_ref, gout_ref, gate_ref, o_ref, 1.0, tm)


def _proj_out(x, ya, yb, mod5, norm4, w_out, l):
    tm = OUT_TM
    half = D_MODEL // 2
    return pl.pallas_call(
        functools.partial(_proj_out_kernel, tm=tm),
        grid=(N_TOK // tm,),
        in_specs=[
            pl.BlockSpec((tm, D_MODEL), lambda i: (i, 0)),
            pl.BlockSpec((tm, half), lambda i: (i, 0)),
            pl.BlockSpec((tm, half), lambda i: (i, 0)),
            _mod_spec(l, 5, tm),
            _gain_spec(l, 3),
            pl.BlockSpec((half, D_MODEL), lambda i: (0, 0)),
            pl.BlockSpec((half, D_MODEL), lambda i: (1, 0)),
        ],
        out_specs=pl.BlockSpec((tm, D_MODEL), lambda i: (i, 0)),
        out_shape=jax.ShapeDtypeStruct((N_TOK, D_MODEL), F32),
        scratch_shapes=[pltpu.VMEM((tm, D_MODEL), F32)],
        compiler_params=_cparams(("parallel",)),
        name="proj_out",
    )(x, ya, yb, mod5, norm4, w_out, w_out)


def _s5_operators(lam_re, lam_im, log_dt, b_re, b_im, c_re, c_im):
    T, G, P, K = S5_T, S5_GROUPS, S5_P, S5_GROUP
    lam = lax.complex(jnp.minimum(lam_re.astype(F32), -1e-4), lam_im.astype(F32))
    ldt = lam * jnp.exp(log_dt.astype(F32))[..., None]
    a = jnp.exp(ldt)
    bbar = ((a - 1.0) / lam)[..., None] * lax.complex(b_re.astype(F32), b_im.astype(F32))
    c = lax.complex(c_re.astype(F32), c_im.astype(F32))
    steps = jnp.arange(T + 1, dtype=F32)
    apow = jnp.exp(steps[:, None, None, None] * ldt[None])
    hi = lax.Precision.HIGHEST
    kern = jnp.real(jnp.einsum('dgcp,jdgp,dgpk->djgck', c, apow[:T], bbar, precision=hi))
    s_idx = jnp.arange(T)[:, None]
    i_idx = jnp.arange(T)[None, :]
    lag = i_idx - s_idx
    kf = jnp.where((lag >= 0)[:, :, None, None, None], kern[0][jnp.clip(lag, 0, T - 1)], 0.0)
    kb = jnp.where((lag <= 0)[:, :, None, None, None], kern[1][jnp.clip(-lag, 0, T - 1)], 0.0)
    m1 = jnp.transpose(kf + kb, (2, 0, 4, 1, 3)).reshape(G, T * K, T * K)
    wf = apow[T - 1 - jnp.arange(T), 0][..., None] * bbar[0][None]
    wb = apow[jnp.arange(T), 1][..., None] * bbar[1][None]
    m2 = jnp.concatenate([jnp.real(wf), jnp.imag(wf), jnp.real(wb), jnp.imag(wb)], axis=2)
    m2 = jnp.transpose(m2, (1, 0, 3, 2)).reshape(G, T * K, 4 * P)
    vf = c[0][None] * apow[1 + jnp.arange(T), 0][:, :, None, :]
    vb = c[1][None] * apow[T - jnp.arange(T), 1][:, :, None, :]
    m3 = jnp.concatenate([jnp.real(vf), -jnp.imag(vf), jnp.real(vb), -jnp.imag(vb)], axis=3)
    m3 = jnp.transpose(m3, (1, 3, 0, 2)).reshape(G, 4 * P, T * K)
    m12 = jnp.concatenate([m1, m2], axis=2).astype(BF16)
    d = (2.0 ** jnp.arange(S5_NPOW, dtype=F32)) * T
    ad = jnp.exp(d[:, None, None, None] * ldt[None])
    ar, ai = jnp.real(ad), jnp.imag(ad)
    prow = jnp.concatenate([ar[:, 0], ar[:, 0], ar[:, 1], ar[:, 1]], axis=-1)
    qrow = jnp.concatenate([-ai[:, 0], ai[:, 0], -ai[:, 1], ai[:, 1]], axis=-1)
    pq = jnp.stack([prow, qrow], axis=1).reshape(2 * S5_NPOW, G, 4 * P)
    pq = jnp.concatenate([pq, jnp.zeros((16 - 2 * S5_NPOW, G, 4 * P), F32)], axis=0)
    return m12, m3.astype(BF16), jnp.transpose(pq, (1, 0, 2))


def _cmul(x, p, q):
    return x * p + pltpu.roll(x, 64, axis=1) * q


def _s5_scan_segment(inc, pq_ref, h0, lc):
    rows = inc.shape[0]
    ridx = lax.broadcasted_iota(jnp.int32, (rows, 128), 0)
    cidx = ridx & (lc - 1)
    fwd, bwd = inc[:, :128], inc[:, 128:]
    if h0 is not None:
        h0f = jnp.where(ridx < lc, h0[0:1, :128], h0[1:2, :128])
        h0b = jnp.where(ridx < lc, h0[0:1, 128:], h0[1:2, 128:])
        fwd = fwd + jnp.where(cidx == 0, _cmul(h0f, pq_ref[0:1, :128], pq_ref[1:2, :128]), 0.0)
        bwd = bwd + jnp.where(cidx == lc - 1, _cmul(h0b, pq_ref[0:1, 128:], pq_ref[1:2, 128:]), 0.0)
    k, d = 0, 1
    while d < lc:
        sh = jnp.where(cidx >= d, pltpu.roll(fwd, d, axis=0), 0.0)
        fwd = fwd + _cmul(sh, pq_ref[2 * k:2 * k + 1, :128], pq_ref[2 * k + 1:2 * k + 2, :128])
        sh = jnp.where(cidx < lc - d, pltpu.roll(bwd, rows - d, axis=0), 0.0)
        bwd = bwd + _cmul(sh, pq_ref[2 * k:2 * k + 1, 128:], pq_ref[2 * k + 1:2 * k + 2, 128:])
        k, d = k + 1, 2 * d
    first_f = 0.0 if h0 is None else h0f
    first_b = 0.0 if h0 is None else h0b
    start_f = jnp.where(cidx >= 1, pltpu.roll(fwd, 1, axis=0), first_f)
    start_b = jnp.where(cidx < lc - 1, pltpu.roll(bwd, rows - 1, axis=0), first_b)
    return jnp.concatenate([start_f, start_b], axis=1), fwd, bwd


def _s5_kernel(x_ref, m12_ref, m3_ref, pq_ref, h0_ref, y_ref, fin_ref, sf_ref, sb_ref):
    z = _dot(x_ref[...].astype(BF16), m12_ref[...])
    nc = S5_CTX_ROWS
    start, fwd, bwd = _s5_scan_segment(z[:nc, S5_LANES:], pq_ref, None, S5_CTX_CHUNKS)
    y_ref[:nc, :] = z[:nc, :S5_LANES] + _dot(start.astype(BF16), m3_ref[...])
    sf_ref[...] = fwd
    sb_ref[...] = bwd
    fin_ref[:, :128] = sf_ref[pl.ds(S5_CTX_CHUNKS - 1, BATCH, stride=S5_CTX_CHUNKS), :]
    fin_ref[:, 128:] = sb_ref[pl.ds(0, BATCH, stride=S5_CTX_CHUNKS), :]
    start, _, _ = _s5_scan_segment(z[nc:, S5_LANES:], pq_ref, h0_ref[...], S5_DEC_CHUNKS)
    y_ref[nc:, :] = z[nc:, :S5_LANES] + _dot(start.astype(BF16), m3_ref[...])


def _s5_scan(u_chunks, m12, m3, pq, h0):
    G = S5_GROUPS
    return pl.pallas_call(
        _s5_kernel,
        grid=(G,),
        in_specs=[
            pl.BlockSpec((None, S5_ROWS, S5_LANES), lambda g: (g, 0, 0)),
            pl.BlockSpec((None, S5_LANES, 2 * S5_LANES), lambda g: (g, 0, 0)),
            pl.BlockSpec((None, S5_LANES, S5_LANES), lambda g: (g, 0, 0)),
            pl.BlockSpec((None, 16, S5_LANES), lambda g: (g, 0, 0)),
            pl.BlockSpec((None, 8, S5_LANES), lambda g: (g, 0, 0)),
        ],
        out_specs=[
            pl.BlockSpec((None, S5_ROWS, S5_LANES), lambda g: (g, 0, 0)),
            pl.BlockSpec((None, BATCH, S5_LANES), lambda g: (g, 0, 0)),
        ],
        out_shape=[
            jax.ShapeDtypeStruct((G, S5_ROWS, S5_LANES), F32),
            jax.ShapeDtypeStruct((G, BATCH, S5_LANES), F32),
        ],
        scratch_shapes=[pltpu.VMEM((S5_CTX_ROWS, 128), F32), pltpu.VMEM((S5_CTX_ROWS, 128), F32)],
        compiler_params=_cparams(("parallel",)),
        name="s5_scan",
    )(u_chunks, m12, m3, pq, h0)


S5_POST_TM = 512


def _s5_post_kernel(y_ref, u_ref, d_ref, w_ref, b_ref, o_ref):
    y = y_ref[...] + d_ref[...] * u_ref[...]
    z = 0.5 * y * (1.0 + jnp.tanh(math.sqrt(2.0 / math.pi) * (y + 0.044715 * (y * y * y))))
    gate = jax.nn.sigmoid(_dot(z.astype(BF16), w_ref[...]) + b_ref[...])
    o_ref[...] = (z * gate).astype(BF16)


def _s5_post(y, proj, d_skip, w_glu, b_glu):
    tm = S5_POST_TM
    return pl.pallas_call(
        _s5_post_kernel,
        grid=(N_TOK // tm,),
        in_specs=[
            pl.BlockSpec((tm, S5_WIDTH), lambda i: (i, 0)),
            pl.BlockSpec((tm, S5_WIDTH), lambda i: (i, 0)),
            pl.BlockSpec((1, S5_WIDTH), lambda i: (0, 0)),
            pl.BlockSpec((S5_WIDTH, S5_WIDTH), lambda i: (0, 0)),
            pl.BlockSpec((1, S5_WIDTH), lambda i: (0, 0)),
        ],
        out_specs=pl.BlockSpec((tm, S5_WIDTH), lambda i: (i, 0)),
        out_shape=jax.ShapeDtypeStruct((N_TOK, S5_WIDTH), BF16),
        compiler_params=_cparams(("parallel",)),
        name="s5_post",
    )(y, proj, d_skip.reshape(1, S5_WIDTH), w_glu, b_glu.reshape(1, S5_WIDTH))


def _s5_mixer(proj, ops, h0, d_skip, w_glu, b_glu):
    G, T, K = S5_GROUPS, S5_T, S5_GROUP
    m12, m3, pq = ops
    u = proj[:, :S5_WIDTH].reshape(S5_ROWS, T, G, K)
    u_chunks = jnp.transpose(u, (2, 0, 1, 3)).reshape(G, S5_ROWS, S5_LANES)
    y_chunks, fin = _s5_scan(u_chunks, m12, m3, pq, h0)
    y = jnp.transpose(y_chunks.reshape(G, S5_ROWS, T, K), (1, 2, 0, 3)).reshape(N_TOK, S5_WIDTH)
    y_a = _s5_post(y, proj, d_skip, w_glu, b_glu)
    fin = jnp.transpose(fin.reshape(G, BATCH, 2, 2, S5_P), (1, 2, 3, 0, 4))
    return y_a, fin[:, :, 0], fin[:, :, 1]


def _ret_kernel(lg_ref, q_ref, k_ref, v_ref, g_ref, gn_ref, *rest, seq_len, has_state):
    if has_state:
        s0_ref, o_ref = rest
        sfin_ref = None
    else:
        o_ref, sfin_ref = rest
    hd = pl.program_id(1)
    C = RET_CHUNK
    nc = seq_len // C
    lgf = lg_ref[0, hd]
    lgb = lg_ref[1, hd]
    row = lax.broadcasted_iota(jnp.int32, (C, C), 0).astype(F32)
    col = lax.broadcasted_iota(jnp.int32, (C, C), 1).astype(F32)
    rel = row - col
    decay = (jnp.where(rel >= 0, jnp.exp(jnp.maximum(rel, 0.0) * lgf), 0.0)
             + jnp.where(rel <= 0, jnp.exp(jnp.maximum(-rel, 0.0) * lgb), 0.0))
    kdec_f = jnp.exp((C - 1.0 - row) * lgf)
    kdec_b = jnp.exp(row * lgb)
    qdec_f = jnp.exp((row + 1.0) * lgf)
    qdec_b = jnp.exp((C - row) * lgb)
    cd_f = jnp.exp(jnp.full((1, RET_DV), C * lgf, F32))
    cd_b = jnp.exp(jnp.full((1, RET_DV), C * lgb, F32))

    def chunk(ref, c):
        return ref[c * C:(c + 1) * C, :]

    tdot = functools.partial(lax.dot_general, dimension_numbers=(((0,), (0,)), ((), ())),
                             preferred_element_type=F32)
    outs = []
    kv_f, kv_b = [], []
    for c in range(nc):
        q = chunk(q_ref, c)
        k = chunk(k_ref, c) * (RET_DK ** -0.5)
        v = chunk(v_ref, c).astype(BF16)
        inner = lax.dot_general(q.astype(BF16), k.astype(BF16), (((1,), (1,)), ((), ())),
                                preferred_element_type=F32) * decay
        outs.append(_dot(inner.astype(BF16), v))
        kv_f.append(tdot((k * kdec_f).astype(BF16), v))
        kv_b.append(tdot((k * kdec_b).astype(BF16), v))
    if has_state:
        s = s0_ref[0]
    else:
        s = jnp.zeros((RET_DK, RET_DV), F32)
    for c in range(nc):
        outs[c] = outs[c] + _dot((chunk(q_ref, c) * qdec_f).astype(BF16), s.astype(BF16))
        s = cd_f * s + kv_f[c]
    if not has_state:
        sfin_ref[0] = s
    if has_state:
        s = s0_ref[1]
    else:
        s = jnp.zeros((RET_DK, RET_DV), F32)
    for c in reversed(range(nc)):
        outs[c] = outs[c] + _dot((chunk(q_ref, c) * qdec_b).astype(BF16), s.astype(BF16))
        s = cd_b * s + kv_b[c]
    if not has_state:
        sfin_ref[1] = s
    for c in range(nc):
        o = outs[c]
        mu = jnp.mean(o, axis=-1, keepdims=True)
        var = jnp.mean(jnp.square(o - mu), axis=-1, keepdims=True)
        o = ((o - mu) * lax.rsqrt(var + EPS)) * gn_ref[...]
        g = chunk(g_ref, c)
        o_ref[c * C:(c + 1) * C, :] = ((g * jax.nn.sigmoid(g)) * o).astype(BF16)


def _retention(proj, lg, gn_g, s0, n_seq, seq_len, blk0):
    H = RET_HEADS
    base = S5_WIDTH // RET_DK

    def col_spec(off):
        return pl.BlockSpec((seq_len, RET_DK), lambda b, h, off=off: (blk0 + b, base + off * H + h))

    in_specs = [
        pl.BlockSpec(memory_space=pltpu.SMEM),
        col_spec(0), col_spec(1), col_spec(2), col_spec(3),
        pl.BlockSpec((1, RET_DV), lambda b, h: (0, h)),
    ]
    args = [lg, proj, proj, proj, proj, gn_g.reshape(1, RET_WIDTH)]
    out_specs = [pl.BlockSpec((seq_len, RET_DV), lambda b, h: (b, h))]
    out_shape = [jax.ShapeDtypeStruct((n_seq * seq_len, RET_WIDTH), BF16)]
    if s0 is not None:
        in_specs.append(pl.BlockSpec((None, 2, None, RET_DK, RET_DV), lambda b, h: (b, 0, h, 0, 0)))
        args.append(s0)
    else:
        out_specs.append(pl.BlockSpec((None, 2, None, RET_DK, RET_DV), lambda b, h: (b, 0, h, 0, 0)))
        out_shape.append(jax.ShapeDtypeStruct((n_seq, 2, H, RET_DK, RET_DV), F32))
    return pl.pallas_call(
        functools.partial(_ret_kernel, seq_len=seq_len, has_state=s0 is not None),
        grid=(n_seq, H),
        in_specs=in_specs,
        out_specs=out_specs,
        out_shape=out_shape,
        compiler_params=_cparams(("parallel", "parallel")),
        name="retention_dec" if s0 is not None else "retention_ctx",
    )(*args)


def _shift_rows(x, d, ridx):
    n = x.shape[0]
    if d == 0:
        return x
    rolled = pltpu.roll(x, d % n, axis=0)
    valid = (ridx >= d) if d > 0 else (ridx < n + d)
    return jnp.where(valid, rolled, 0.0)


def _pool_kernel(u_ref, w_ref, sc_ref, o_ref, *, seq_len):
    L = seq_len
    ridx = lax.broadcasted_iota(jnp.int32, (L, POOL_GC), 0)
    for gi, w in enumerate(POOL_WINDOWS):
        cols = slice(gi * POOL_GC, (gi + 1) * POOL_GC)
        x = u_ref[:, cols]
        half = w // 2
        back, ahead = x, x
        span = 1
        while span < half:
            back = back + _shift_rows(back, span, ridx)
            ahead = ahead + _shift_rows(ahead, -span, ridx)
            span *= 2
        win = _shift_rows(back, 1, ridx) + ahead
        lo = jnp.maximum(ridx - half, 0)
        hi = jnp.minimum(ridx + half - 1, L - 1)
        mean = win / (hi - lo + 1).astype(F32)
        mixed = _dot((mean - x).astype(BF16), w_ref[gi])
        o_ref[:, cols] = (mixed * sc_ref[:, cols]).astype(BF16)


def _pool(proj, w_pool, pool_scale, n_seq, seq_len, blk0):
    return pl.pallas_call(
        functools.partial(_pool_kernel, seq_len=seq_len),
        grid=(n_seq,),
        in_specs=[
            pl.BlockSpec((seq_len, POOL_WIDTH), lambda b: (blk0 + b, 0)),
            pl.BlockSpec((POOL_GROUPS, POOL_GC, POOL_GC), lambda b: (0, 0, 0)),
            pl.BlockSpec((1, POOL_WIDTH), lambda b: (0, 0)),
        ],
        out_specs=pl.BlockSpec((seq_len, POOL_WIDTH), lambda b: (b, 0)),
        out_shape=jax.ShapeDtypeStruct((n_seq * seq_len, POOL_WIDTH), BF16),
        compiler_params=_cparams(("parallel",)),
        name="pool",
    )(proj, w_pool, pool_scale.reshape(1, POOL_WIDTH))


def _softmax_pv(s, v, sink):
    m = jnp.maximum(jnp.max(s, axis=-1, keepdims=True), sink)
    p = jnp.exp(s - m)
    denom = jnp.sum(p, axis=-1, keepdims=True) + jnp.exp(sink - m)
    return _dot(p.astype(BF16), v) / denom


def _qk(q, k):
    return lax.dot_general(q, k, (((1,), (1,)), ((), ())), preferred_element_type=F32) * (ATT_HD ** -0.5)


def _ctx_att_kernel(sink_ref, q_ref, k_ref, v_ref, o_ref):
    kv = pl.program_id(1)
    k = k_ref[...].astype(BF16)
    v = v_ref[...].astype(BF16)
    for g in range(ATT_GROUP):
        cols = slice(g * ATT_HD, (g + 1) * ATT_HD)
        s = _qk(q_ref[:, cols].astype(BF16), k)
        o_ref[:, cols] = _softmax_pv(s, v, sink_ref[kv * ATT_GROUP + g]).astype(BF16)


def _ctx_attention(proj, sink):
    qw = ATT_GROUP * ATT_HD
    qb = POOL_WIDTH // qw
    kb = (POOL_WIDTH + ATT_HEADS * ATT_HD) // ATT_HD
    return pl.pallas_call(
        _ctx_att_kernel,
        grid=(BATCH, ATT_KV),
        in_specs=[
            pl.BlockSpec(memory_space=pltpu.SMEM),
            pl.BlockSpec((SEQ, qw), lambda b, kv: (b, qb + kv)),
            pl.BlockSpec((SEQ, ATT_HD), lambda b, kv: (b, kb + kv)),
            pl.BlockSpec((SEQ, ATT_HD), lambda b, kv: (b, kb + ATT_KV + kv)),
        ],
        out_specs=pl.BlockSpec((SEQ, qw), lambda b, kv: (b, kv)),
        out_shape=jax.ShapeDtypeStruct((N_CTX_TOK, ATT_HEADS * ATT_HD), BF16),
        compiler_params=_cparams(("parallel", "parallel")),
        name="attention_ctx",
    )(sink, proj, proj, proj)


def _rope(x, cos, sin):
    lane = lax.broadcasted_iota(jnp.int32, x.shape, 1)
    partner = jnp.where((lane & 63) < 32, pltpu.roll(x, 96, axis=1), pltpu.roll(x, 32, axis=1))
    return x * cos + partner * sin


def _dec_att_kernel(sink_ref, q_ref, k_ref, v_ref, kc_ref, vc_ref, cos_ref, sin_ref, o_ref, kr_ref):
    kv = pl.program_id(1)
    L, B = DEC_SEQ, ATT_BLOCK
    nb = L // B
    kr_ref[...] = _rope(k_ref[...], cos_ref[...], sin_ref[...]).astype(BF16)
    kc = kc_ref[...].astype(BF16)
    vc = vc_ref[...].astype(BF16)
    for i in range(nb):
        lo, hi = max(i - 1, 0) * B, min(i + 2, nb) * B
        rows = slice(i * B, (i + 1) * B)
        kwin = kr_ref[lo:hi, :]
        vals = jnp.concatenate([v_ref[lo:hi, :].astype(BF16), vc], axis=0)
        qpos = i * B + lax.broadcasted_iota(jnp.int32, (B, hi - lo), 0)
        kpos = lo + lax.broadcasted_iota(jnp.int32, (B, hi - lo), 1)
        band = jnp.abs(kpos - qpos) <= ATT_WIN
        for g in range(ATT_GROUP):
            cols = slice(g * ATT_HD, (g + 1) * ATT_HD)
            q = _rope(q_ref[rows, cols], cos_ref[rows, :], sin_ref[rows, :]).astype(BF16)
            s = jnp.concatenate([jnp.where(band, _qk(q, kwin), NEG), _qk(q, kc)], axis=1)
            o_ref[rows, cols] = _softmax_pv(s, vals, sink_ref[kv * ATT_GROUP + g]).astype(BF16)


def _rope_tables():
    half = ATT_HD // 2
    nfreq = half // 2
    freqs = ROPE_BASE ** (-jnp.arange(nfreq, dtype=F32) / nfreq)
    t = jnp.arange(DEC_SEQ)
    pos = jnp.stack([t // GRID_W, t % GRID_W], axis=1).astype(F32)
    ang = pos[:, :, None] * freqs[None, None, :]
    cos = jnp.repeat(jnp.cos(ang), 2, axis=1).reshape(DEC_SEQ, ATT_HD)
    sin = jnp.sin(ang)
    sin = jnp.stack([-sin, sin], axis=2).reshape(DEC_SEQ, ATT_HD)
    return cos, sin


def _dec_attention(proj, cache_k, cache_v, sink, cos, sin, o):
    qw = ATT_GROUP * ATT_HD
    qb = POOL_WIDTH // qw
    kb = (POOL_WIDTH + ATT_HEADS * ATT_HD) // ATT_HD
    blk0 = N_CTX_TOK // DEC_SEQ
    ck = cache_k.reshape(DEC_BATCH, N_ODD, PAST_LEN, ATT_KV * ATT_HD)
    cv = cache_v.reshape(DEC_BATCH, N_ODD, PAST_LEN, ATT_KV * ATT_HD)
    cache_spec = pl.BlockSpec((None, None, PAST_LEN, ATT_HD), lambda b, kv: (b, o, 0, kv))
    table_spec = pl.BlockSpec((DEC_SEQ, ATT_HD), lambda b, kv: (0, 0))
    return pl.pallas_call(
        _dec_att_kernel,
        grid=(DEC_BATCH, ATT_KV),
        in_specs=[
            pl.BlockSpec(memory_space=pltpu.SMEM),
            pl.BlockSpec((DEC_SEQ, qw), lambda b, kv: (blk0 + b, qb + kv)),
            pl.BlockSpec((DEC_SEQ, ATT_HD), lambda b, kv: (blk0 + b, kb + kv)),
            pl.BlockSpec((DEC_SEQ, ATT_HD), lambda b, kv: (blk0 + b, kb + ATT_KV + kv)),
            cache_spec, cache_spec, table_spec, table_spec,
        ],
        out_specs=pl.BlockSpec((DEC_SEQ, qw), lambda b, kv: (b, kv)),
        out_shape=jax.ShapeDtypeStruct((N_DEC_TOK, ATT_HEADS * ATT_HD), BF16),
        scratch_shapes=[pltpu.VMEM((DEC_SEQ, ATT_HD), BF16)],
        compiler_params=_cparams(("parallel", "parallel")),
        name="attention_dec",
    )(sink, proj, proj, proj, ck, cv, cos, sin)


def kernel(x_prompt, x_sample, c, state_s5_re, state_s5_im, state_ret, cache_k, cache_v, c_ctx, w_mod, b_mod, norm_g, ffn1_gate, ffn1_up, ffn1_down, ffn2_gate, ffn2_up, ffn2_down, even_w_in, even_w_out, s5_lam_re, s5_lam_im, s5_log_dt, s5_b_re, s5_b_im, s5_c_re, s5_c_im, s5_d, s5_glu_w, s5_glu_b, ret_decay_logit, ret_gn_g, odd_w_in, odd_w_out, pool_w, pool_scale, att_sink):
    x = jnp.concatenate([x_prompt.reshape(N_CTX_TOK, D_MODEL), x_sample.reshape(N_DEC_TOK, D_MODEL)], axis=0)
    c8 = jnp.concatenate([c_ctx[None, :], c, jnp.zeros((MOD_ROWS - 1 - DEC_BATCH, D_MODEL), F32)], axis=0)
    mod5 = _modulation(c8, w_mod, b_mod).reshape(DEPTH, MOD_ROWS, N_MOD, 1, D_MODEL)
    norm4 = norm_g.reshape(DEPTH, 6, 1, D_MODEL)
    cos, sin = _rope_tables()
    dec_blk = N_CTX_TOK // DEC_SEQ
    out_re, out_im, out_ret, out_k, out_v = [], [], [], [], []
    for l in range(DEPTH):
        x = _ffn(x, mod5, norm4, ffn1_gate, ffn1_up, ffn1_down, l, 0, 0, 1)
        if l % 2 == 0:
            e = l // 2
            proj = _proj_in(x, mod5, norm4, even_w_in[e].astype(BF16), l)
            ops = _s5_operators(s5_lam_re[e], s5_lam_im[e], s5_log_dt[e], s5_b_re[e], s5_b_im[e],
                                s5_c_re[e], s5_c_im[e])
            h0 = jnp.stack([state_s5_re[:, e], state_s5_im[:, e]], axis=2)
            h0 = jnp.transpose(h0, (3, 0, 1, 2, 4)).reshape(S5_GROUPS, DEC_BATCH, S5_LANES)
            h0 = jnp.concatenate([h0, jnp.zeros((S5_GROUPS, 8 - DEC_BATCH, S5_LANES), F32)], axis=1)
            y_a, s_re, s_im = _s5_mixer(proj, ops, h0, s5_d[e], s5_glu_w[e].astype(BF16), s5_glu_b[e])
            lg = jax.nn.log_sigmoid(ret_decay_logit[e].astype(F32))
            yb_ctx, s_ret = _retention(proj, lg, ret_gn_g[e], None, BATCH, SEQ, 0)
            (yb_dec,) = _retention(proj, lg, ret_gn_g[e], state_ret[:, e], DEC_BATCH, DEC_SEQ, dec_blk)
            y_b = jnp.concatenate([yb_ctx, yb_dec], axis=0)
            out_re.append(s_re)
            out_im.append(s_im)
            out_ret.append(s_ret)
            w_out = even_w_out[e]
        else:
            o = l // 2
            proj = _proj_in(x, mod5, norm4, odd_w_in[o].astype(BF16), l)
            w_pool = pool_w[o].astype(BF16)
            y_a = jnp.concatenate([_pool(proj, w_pool, pool_scale[o], BATCH, SEQ, 0),
                                   _pool(proj, w_pool, pool_scale[o], DEC_BATCH, DEC_SEQ, dec_blk)], axis=0)
            sink = att_sink[o].astype(F32)
            y_b = jnp.concatenate([_ctx_attention(proj, sink),
                                   _dec_attention(proj, cache_k, cache_v, sink, cos, sin, o)], axis=0)
            kv0 = POOL_WIDTH + ATT_HEADS * ATT_HD
            kv1 = kv0 + ATT_KV * ATT_HD
            out_k.append(proj[:N_CTX_TOK, kv0:kv1].reshape(BATCH, SEQ, ATT_KV, ATT_HD))
            out_v.append(proj[:N_CTX_TOK, kv1:].reshape(BATCH, SEQ, ATT_KV, ATT_HD))
            w_out = odd_w_out[o]
        x = _proj_out(x, y_a, y_b, mod5, norm4, w_out.astype(BF16), l)
        x = _ffn(x, mod5, norm4, ffn2_gate, ffn2_up, ffn2_down, l, 2, 4, 5)
    y_prompt = x[:N_CTX_TOK].reshape(BATCH, SEQ, D_MODEL)
    y_sample = x[N_CTX_TOK:].reshape(DEC_BATCH, DEC_SEQ, D_MODEL)
    return (y_prompt, y_sample, jnp.stack(out_re, axis=1), jnp.stack(out_im, axis=1),
            jnp.stack(out_ret, axis=1), jnp.stack(out_k, axis=1), jnp.stack(out_v, axis=1))
```

```python
import functools
import math

import jax
import jax.numpy as jnp
from jax import lax
from jax.experimental import pallas as pl
from jax.experimental.pallas import tpu as pltpu

F32 = jnp.float32
BF16 = jnp.bfloat16

D_MODEL = 2048
BATCH = 16
SEQ = 256
DEPTH = 4
DEC_BATCH = 2
DEC_SEQ = 1024
PAST_LEN = 256
GRID_W = 64
N_EVEN = (DEPTH + 1) // 2
N_ODD = DEPTH // 2
N_MOD = 9
FFN_HIDDEN = 5632
EPS = 1e-6
NEG = -1e30

S5_WIDTH = D_MODEL // 2
S5_GROUP = 16
S5_GROUPS = S5_WIDTH // S5_GROUP
S5_P = 64
RET_HEADS = 8
RET_DK = (D_MODEL // 2) // RET_HEADS
RET_DV = RET_DK
RET_WIDTH = RET_HEADS * RET_DV
RET_CHUNK = 128
EVEN_IN = S5_WIDTH + 4 * RET_WIDTH

POOL_WIDTH = D_MODEL // 2
POOL_WINDOWS = (2, 4, 8, 16)
POOL_GROUPS = 4
POOL_GC = POOL_WIDTH // POOL_GROUPS
ATT_HEADS = 8
ATT_KV = 2
ATT_HD = (D_MODEL // 2) // ATT_HEADS
ATT_GROUP = ATT_HEADS // ATT_KV
ATT_WIN = 128
ATT_BLOCK = 128
ODD_IN = POOL_WIDTH + (ATT_HEADS + 2 * ATT_KV) * ATT_HD
ROPE_BASE = 10000.0

N_CTX_TOK = BATCH * SEQ
N_DEC_TOK = DEC_BATCH * DEC_SEQ
N_TOK = N_CTX_TOK + N_DEC_TOK
MOD_ROWS = 8

S5_T = 16
S5_LANES = S5_T * S5_GROUP
S5_CTX_CHUNKS = SEQ // S5_T
S5_DEC_CHUNKS = DEC_SEQ // S5_T
S5_CTX_ROWS = BATCH * S5_CTX_CHUNKS
S5_DEC_ROWS = DEC_BATCH * S5_DEC_CHUNKS
S5_ROWS = S5_CTX_ROWS + S5_DEC_ROWS
S5_NPOW = 6

VMEM_LIMIT = 60 * 1024 * 1024


def _cparams(sem):
    return pltpu.CompilerParams(dimension_semantics=sem, vmem_limit_bytes=VMEM_LIMIT)


def _dot(a, b):
    return jnp.dot(a, b, preferred_element_type=F32)


def _rms(x, g):
    return (x * lax.rsqrt(jnp.mean(x * x, axis=-1, keepdims=True) + EPS)) * g


def _mod_row(i, tm):
    start = i * tm
    return (start >= N_CTX_TOK).astype(jnp.int32) + (start >= N_CTX_TOK + DEC_SEQ).astype(jnp.int32)


def _mod_kernel(c_ref, w_ref, b_ref, o_ref):
    c = c_ref[...]
    s = (c * jax.nn.sigmoid(c)).astype(BF16)
    o_ref[...] = _dot(s, w_ref[...].astype(BF16)) + b_ref[...]


def _modulation(c8, w_mod, b_mod):
    tn = 1024
    n = N_MOD * D_MODEL
    return pl.pallas_call(
        _mod_kernel,
        grid=(DEPTH, n // tn),
        in_specs=[
            pl.BlockSpec((MOD_ROWS, D_MODEL), lambda l, j: (0, 0)),
            pl.BlockSpec((None, D_MODEL, tn), lambda l, j: (l, 0, j)),
            pl.BlockSpec((None, 1, tn), lambda l, j: (l, 0, j)),
        ],
        out_specs=pl.BlockSpec((None, MOD_ROWS, tn), lambda l, j: (l, 0, j)),
        out_shape=jax.ShapeDtypeStruct((DEPTH, MOD_ROWS, n), F32),
        compiler_params=_cparams(("arbitrary", "arbitrary")),
        name="modulation",
    )(c8, w_mod, b_mod.reshape(DEPTH, 1, n))


def _mod_spec(l, j, tm):
    return pl.BlockSpec((None, None, None, 1, D_MODEL),
                        lambda i, *_: (l, _mod_row(i, tm), j, 0, 0))


def _gain_spec(l, j):
    return pl.BlockSpec((None, None, 1, D_MODEL), lambda i, *_: (l, j, 0, 0))


FFN_TM = 1024
FFN_TF = 256
ROW_CHUNK = 64


def _adaln_in_rows(x_ref, g_ref, sc_ref, sh_ref, h_ref, tm):
    def body(r, _):
        rows = pl.ds(pl.multiple_of(r * ROW_CHUNK, ROW_CHUNK), ROW_CHUNK)
        h = _rms(x_ref[rows, :], g_ref[...])
        h_ref[rows, :] = (h * (1.0 + sc_ref[...]) + sh_ref[...]).astype(BF16)
        return 0
    lax.fori_loop(0, tm // ROW_CHUNK, body, 0)


def _adaln_out_rows(x_ref, y_ref, g_ref, gate_ref, o_ref, weight, tm):
    def body(r, _):
        rows = pl.ds(pl.multiple_of(r * ROW_CHUNK, ROW_CHUNK), ROW_CHUNK)
        o_ref[rows, :] = x_ref[rows, :] + (weight * gate_ref[...]) * _rms(y_ref[rows, :], g_ref[...])
        return 0
    lax.fori_loop(0, tm // ROW_CHUNK, body, 0)


def _ffn_kernel(x_ref, sh_ref, sc_ref, gate_ref, gin_ref, gout_ref, wg_ref, wu_ref, wd_ref,
                o_ref, h_ref, *, nf, tm):
    f = pl.program_id(1)

    @pl.when(f == 0)
    def _():
        _adaln_in_rows(x_ref, gin_ref, sc_ref, sh_ref, h_ref, tm)
        o_ref[...] = jnp.zeros_like(o_ref)

    h = h_ref[...]
    g = _dot(h, wg_ref[...].astype(BF16))
    u = _dot(h, wu_ref[...].astype(BF16))
    a = ((g * jax.nn.sigmoid(g)) * u).astype(BF16)
    o_ref[...] += _dot(a, wd_ref[...].astype(BF16))

    @pl.when(f == nf - 1)
    def _():
        _adaln_out_rows(x_ref, o_ref, gout_ref, gate_ref, o_ref, 0.5, tm)


def _ffn(x, mod5, norm4, w_gate, w_up, w_down, l, j, g_in, g_out):
    tm, tf = FFN_TM, FFN_TF
    nf = FFN_HIDDEN // tf
    return pl.pallas_call(
        functools.partial(_ffn_kernel, nf=nf, tm=tm),
        grid=(N_TOK // tm, nf),
        in_specs=[
            pl.BlockSpec((tm, D_MODEL), lambda i, f: (i, 0)),
            _mod_spec(l, 3 * j, tm), _mod_spec(l, 3 * j + 1, tm), _mod_spec(l, 3 * j + 2, tm),
            _gain_spec(l, g_in), _gain_spec(l, g_out),
            pl.BlockSpec((None, D_MODEL, tf), lambda i, f: (l, 0, f)),
            pl.BlockSpec((None, D_MODEL, tf), lambda i, f: (l, 0, f)),
            pl.BlockSpec((None, tf, D_MODEL), lambda i, f: (l, f, 0)),
        ],
        out_specs=pl.BlockSpec((tm, D_MODEL), lambda i, f: (i, 0)),
        out_shape=jax.ShapeDtypeStruct((N_TOK, D_MODEL), F32),
        scratch_shapes=[pltpu.VMEM((tm, D_MODEL), BF16)],
        compiler_params=_cparams(("parallel", "arbitrary")),
        name="ffn",
    )(x, mod5, mod5, mod5, norm4, norm4, w_gate, w_up, w_down)


PROJ_TM = 1024
PROJ_TN = 1280


def _proj_in_kernel(x_ref, sh_ref, sc_ref, gin_ref, w_ref, o_ref, h_ref, *, tm):
    @pl.when(pl.program_id(1) == 0)
    def _():
        _adaln_in_rows(x_ref, gin_ref, sc_ref, sh_ref, h_ref, tm)

    o_ref[...] = _dot(h_ref[...], w_ref[...])


def _proj_in(x, mod5, norm4, w_in, l):
    tm, tn = PROJ_TM, PROJ_TN
    n = w_in.shape[-1]
    return pl.pallas_call(
        functools.partial(_proj_in_kernel, tm=tm),
        grid=(N_TOK // tm, n // tn),
        in_specs=[
            pl.BlockSpec((tm, D_MODEL), lambda i, k: (i, 0)),
            _mod_spec(l, 3, tm), _mod_spec(l, 4, tm),
            _gain_spec(l, 2),
            pl.BlockSpec((D_MODEL, tn), lambda i, k: (0, k)),
        ],
        out_specs=pl.BlockSpec((tm, tn), lambda i, k: (i, k)),
        out_shape=jax.ShapeDtypeStruct((N_TOK, n), F32),
        scratch_shapes=[pltpu.VMEM((tm, D_MODEL), BF16)],
        compiler_params=_cparams(("parallel", "arbitrary")),
        name="proj_in",
    )(x, mod5, mod5, norm4, w_in)


OUT_TM = 512


OUT_ROWS = 256


def _proj_out_kernel(x_ref, yac_ref, yad_ref, ybc_ref, ybd_ref, gate_ref, gout_ref, w_ref, o_ref, *, tm):
    def run(ya_ref, yb_ref):
        for r0 in range(0, tm, OUT_ROWS):
            rows = slice(r0, r0 + OUT_ROWS)
            y = _dot(jnp.concatenate([ya_ref[rows, :], yb_ref[rows, :]], axis=1), w_ref[...])
            o_ref[rows, :] = x_ref[rows, :] + gate_ref[...] * _rms(y, gout_ref[...])

    is_ctx = pl.program_id(0) < N_CTX_TOK // tm
    pl.when(is_ctx)(lambda: run(yac_ref, ybc_ref))
    pl.when(jnp.logical_not(is_ctx))(lambda: run(yad_ref, ybd_ref))


def _proj_out(x, ya, yb, mod5, norm4, w_out, l):
    tm = OUT_TM
    half = D_MODEL // 2
    nct = N_CTX_TOK // tm

    def split_specs(y):
        ctx_spec = pl.BlockSpec((tm, half), lambda i: (jnp.minimum(i, nct - 1), 0))
        if isinstance(y, tuple):
            return y, (ctx_spec, pl.BlockSpec((tm, half), lambda i: (jnp.maximum(i - nct, 0), 0)))
        return (y, y), (ctx_spec, pl.BlockSpec((tm, half), lambda i: (jnp.maximum(i, nct), 0)))

    (ya_c, ya_d), (sa_c, sa_d) = split_specs(ya)
    (yb_c, yb_d), (sb_c, sb_d) = split_specs(yb)
    return pl.pallas_call(
        functools.partial(_proj_out_kernel, tm=tm),
        grid=(N_TOK // tm,),
        in_specs=[
            pl.BlockSpec((tm, D_MODEL), lambda i: (i, 0)),
            sa_c, sa_d, sb_c, sb_d,
            _mod_spec(l, 5, tm),
            _gain_spec(l, 3),
            pl.BlockSpec((D_MODEL, D_MODEL), lambda i: (0, 0)),
        ],
        out_specs=pl.BlockSpec((tm, D_MODEL), lambda i: (i, 0)),
        out_shape=jax.ShapeDtypeStruct((N_TOK, D_MODEL), F32),
        compiler_params=_cparams(("parallel",)),
        name="proj_out",
    )(x, ya_c, ya_d, yb_c, yb_d, mod5, norm4, w_out)


def _s5_operators(lam_re, lam_im, log_dt, b_re, b_im, c_re, c_im):
    T, G, P, K = S5_T, S5_GROUPS, S5_P, S5_GROUP
    lam = lax.complex(jnp.minimum(lam_re.astype(F32), -1e-4), lam_im.astype(F32))
    ldt = lam * jnp.exp(log_dt.astype(F32))[..., None]
    a = jnp.exp(ldt)
    bbar = ((a - 1.0) / lam)[..., None] * lax.complex(b_re.astype(F32), b_im.astype(F32))
    c = lax.complex(c_re.astype(F32), c_im.astype(F32))
    s = jnp.arange(T, dtype=F32)

    def powers(n, d):
        return jnp.exp(n[:, None, None] * ldt[d][None])

    def state_rows(wf, wb):
        w = jnp.concatenate([jnp.real(wf), jnp.imag(wf), jnp.real(wb), jnp.imag(wb)], axis=2)
        return jnp.transpose(w, (1, 2, 0, 3)).reshape(G, 4 * P, T * K)

    inc = state_rows(powers(T - 1.0 - s, 0)[..., None] * bbar[0][None],
                     powers(s, 1)[..., None] * bbar[1][None])
    kin = state_rows(powers(-1.0 - s, 0)[..., None] * bbar[0][None],
                     powers(s - T, 1)[..., None] * bbar[1][None])
    vf = c[0][None] * powers(s + 1.0, 0)[:, :, None, :]
    vb = c[1][None] * powers(T - s, 1)[:, :, None, :]
    out = jnp.concatenate([jnp.real(vf), -jnp.imag(vf), jnp.real(vb), -jnp.imag(vb)], axis=3)
    out = jnp.transpose(out, (1, 0, 2, 3)).reshape(G, T * K, 4 * P)
    a_t = jnp.exp(T * ldt)
    a_rows = jnp.concatenate([jnp.real(a_t[0]), jnp.imag(a_t[0]), jnp.real(a_t[1]), jnp.imag(a_t[1])], axis=1)
    a_rows = jnp.broadcast_to(a_rows[:, :, None], (G, 4 * P, 128))
    return inc, kin, out, a_rows


def _cmul_rows(x, ar, ai):
    h = x.shape[0] // 2
    xr, xi = x[:h], x[h:]
    return jnp.concatenate([xr * ar - xi * ai, xr * ai + xi * ar], axis=0)


def _s5_lane_scan(inc, a_rows, h0, lc):
    P = S5_P
    cidx = lax.broadcasted_iota(jnp.int32, (2 * P, 128), 1) & (lc - 1)
    fwd, bwd = inc[:2 * P], inc[2 * P:]
    arf, aif, arb, aib = a_rows[:P], a_rows[P:2 * P], a_rows[2 * P:3 * P], a_rows[3 * P:]
    if h0 is not None:
        fwd = fwd + jnp.where(cidx == 0, _cmul_rows(h0[:2 * P], arf, aif), 0.0)
        bwd = bwd + jnp.where(cidx == lc - 1, _cmul_rows(h0[2 * P:], arb, aib), 0.0)
    d = 1
    while d < lc:
        sh = jnp.where(cidx >= d, pltpu.roll(fwd, d, axis=1), 0.0)
        fwd = fwd + _cmul_rows(sh, arf, aif)
        sh = jnp.where(cidx < lc - d, pltpu.roll(bwd, 128 - d, axis=1), 0.0)
        bwd = bwd + _cmul_rows(sh, arb, aib)
        d *= 2
        if d < lc:
            arf, aif = arf * arf - aif * aif, 2.0 * arf * aif
            arb, aib = arb * arb - aib * aib, 2.0 * arb * aib
    first_f = 0.0 if h0 is None else h0[:2 * P]
    first_b = 0.0 if h0 is None else h0[2 * P:]
    start_f = jnp.where(cidx >= 1, pltpu.roll(fwd, 1, axis=1), first_f)
    start_b = jnp.where(cidx < lc - 1, pltpu.roll(bwd, 127, axis=1), first_b)
    return jnp.concatenate([start_f, start_b], axis=0), jnp.concatenate([fwd, bwd], axis=0)


def _dot_split(a, b):
    a_hi = a.astype(BF16)
    b_hi = b.astype(BF16)
    a_lo = (a - a_hi.astype(F32)).astype(BF16)
    b_lo = (b - b_hi.astype(F32)).astype(BF16)
    return _dot(a_hi, b_hi) + (_dot(a_hi, b_lo) + _dot(a_lo, b_hi))


S5_SLAB = 128 // S5_GROUP


def _s5_kernel(u_ref, inc_ref, kin_ref, out_ref, a_ref, h0_ref, y_ref, scan_ref, xt_ref, yt_ref):
    T, K, P, R = S5_T, S5_GROUP, S5_P, S5_ROWS
    F = S5_LANES
    for s in range(T):
        ut = u_ref[pl.ds(s, R, stride=T), :].T
        for gg in range(S5_SLAB):
            xt_ref[gg, s * K:(s + 1) * K, :] = ut[gg * K:(gg + 1) * K, :].astype(BF16)
    shift = K.bit_length() - 1
    tok_out = lax.broadcasted_iota(jnp.int32, (F, F), 0) >> shift
    tok_in = lax.broadcasted_iota(jnp.int32, (F, F), 1) >> shift

    def group(gg, carry):
        inc_op = inc_ref[gg]
        kin_op = kin_ref[gg]
        out_op = out_ref[gg]
        resp_f = _dot_split(out_op[:, :2 * P], kin_op[:2 * P])
        resp_b = _dot_split(out_op[:, 2 * P:], kin_op[2 * P:])
        resp = jnp.where(tok_out >= tok_in, resp_f, 0.0) + jnp.where(tok_out <= tok_in, resp_b, 0.0)
        ops = jnp.concatenate([resp, inc_op], axis=0).astype(BF16)
        z = _dot(ops, xt_ref[gg])
        a_rows = a_ref[gg]
        lanes = S5_CTX_ROWS // 2
        st_a, sc_a = _s5_lane_scan(z[F:, :lanes], a_rows, None, S5_CTX_CHUNKS)
        st_b, sc_b = _s5_lane_scan(z[F:, lanes:2 * lanes], a_rows, None, S5_CTX_CHUNKS)
        st_d, _ = _s5_lane_scan(z[F:, 2 * lanes:], a_rows, h0_ref[gg], S5_DEC_CHUNKS)
        start = jnp.concatenate([st_a, st_b, st_d], axis=1).astype(BF16)
        yt_ref[gg] = z[:F] + _dot(out_op.astype(BF16), start)
        scan_ref[gg] = jnp.concatenate([sc_a, sc_b], axis=1)
        return carry

    lax.fori_loop(0, S5_SLAB, group, 0)
    for i in range(T):
        yt = jnp.concatenate([yt_ref[gg, i * K:(i + 1) * K, :] for gg in range(S5_SLAB)], axis=0)
        y_ref[pl.ds(i, R, stride=T), :] = yt.T


def _s5_scan(proj, ops, h0):
    G, F, P4 = S5_GROUPS, S5_LANES, 4 * S5_P
    inc, kin, out, a_rows = ops
    assert S5_CTX_ROWS == 256 and S5_DEC_ROWS == 128 and F == 256 and P4 == 256

    def op_spec(rows, cols):
        return pl.BlockSpec((S5_SLAB, rows, cols), lambda j: (j, 0, 0))

    return pl.pallas_call(
        _s5_kernel,
        grid=(G // S5_SLAB,),
        in_specs=[
            pl.BlockSpec((N_TOK, 128), lambda j: (0, j)),
            op_spec(P4, F), op_spec(P4, F), op_spec(F, P4), op_spec(P4, 128), op_spec(P4, 128),
        ],
        out_specs=[
            pl.BlockSpec((N_TOK, 128), lambda j: (0, j)),
            op_spec(P4, S5_CTX_ROWS),
        ],
        out_shape=[
            jax.ShapeDtypeStruct((N_TOK, S5_WIDTH), F32),
            jax.ShapeDtypeStruct((G, P4, S5_CTX_ROWS), F32),
        ],
        scratch_shapes=[pltpu.VMEM((S5_SLAB, F, S5_ROWS), BF16), pltpu.VMEM((S5_SLAB, F, S5_ROWS), F32)],
        compiler_params=_cparams(("parallel",)),
        name="s5_scan",
    )(proj, inc, kin, out, a_rows, h0)


S5_POST_TM = 512


def _s5_post_kernel(y_ref, u_ref, d_ref, w_ref, b_ref, o_ref):
    y = y_ref[...] + d_ref[...] * u_ref[...]
    z = 0.5 * y * (1.0 + jnp.tanh(math.sqrt(2.0 / math.pi) * (y + 0.044715 * (y * y * y))))
    gate = jax.nn.sigmoid(_dot(z.astype(BF16), w_ref[...]) + b_ref[...])
    o_ref[...] = (z * gate).astype(BF16)


def _s5_post(y, proj, d_skip, w_glu, b_glu):
    tm = S5_POST_TM
    return pl.pallas_call(
        _s5_post_kernel,
        grid=(N_TOK // tm,),
        in_specs=[
            pl.BlockSpec((tm, S5_WIDTH), lambda i: (i, 0)),
            pl.BlockSpec((tm, S5_WIDTH), lambda i: (i, 0)),
            pl.BlockSpec((1, S5_WIDTH), lambda i: (0, 0)),
            pl.BlockSpec((S5_WIDTH, S5_WIDTH), lambda i: (0, 0)),
            pl.BlockSpec((1, S5_WIDTH), lambda i: (0, 0)),
        ],
        out_specs=pl.BlockSpec((tm, S5_WIDTH), lambda i: (i, 0)),
        out_shape=jax.ShapeDtypeStruct((N_TOK, S5_WIDTH), BF16),
        compiler_params=_cparams(("parallel",)),
        name="s5_post",
    )(y, proj, d_skip.reshape(1, S5_WIDTH), w_glu, b_glu.reshape(1, S5_WIDTH))


def _s5_mixer(proj, ops, h0, d_skip, w_glu, b_glu):
    G, P, nc = S5_GROUPS, S5_P, S5_CTX_CHUNKS
    y, scans = _s5_scan(proj, ops, h0)
    y_a = _s5_post(y, proj, d_skip, w_glu, b_glu)
    fin = jnp.stack([scans[:, :2 * P, nc - 1::nc], scans[:, 2 * P:, 0::nc]], axis=0)
    fin = jnp.transpose(fin.reshape(2, G, 2, P, BATCH), (4, 0, 2, 1, 3))
    return y_a, fin[:, :, 0], fin[:, :, 1]


def _ret_head(lgf, lgb, q_ref, k_ref, v_ref, g_ref, gn_ref, o_ref, s0_f, s0_b, cols, seq_len):
    C = RET_CHUNK
    nc = seq_len // C
    row = lax.broadcasted_iota(jnp.int32, (C, C), 0).astype(F32)
    col = lax.broadcasted_iota(jnp.int32, (C, C), 1).astype(F32)
    rel = row - col
    decay = (jnp.where(rel >= 0, jnp.exp(jnp.maximum(rel, 0.0) * lgf), 0.0)
             + jnp.where(rel <= 0, jnp.exp(jnp.maximum(-rel, 0.0) * lgb), 0.0))
    kdec_f = jnp.exp((C - 1.0 - row) * lgf)
    kdec_b = jnp.exp(row * lgb)
    qdec_f = jnp.exp((row + 1.0) * lgf)
    qdec_b = jnp.exp((C - row) * lgb)
    cd_f = jnp.exp(jnp.full((1, RET_DV), C * lgf, F32))
    cd_b = jnp.exp(jnp.full((1, RET_DV), C * lgb, F32))

    def chunk(ref, c):
        return ref[c * C:(c + 1) * C, cols]

    tdot = functools.partial(lax.dot_general, dimension_numbers=(((0,), (0,)), ((), ())),
                             preferred_element_type=F32)
    outs = []
    kv_f, kv_b = [], []
    for c in range(nc):
        q = chunk(q_ref, c)
        k = chunk(k_ref, c) * (RET_DK ** -0.5)
        v = chunk(v_ref, c).astype(BF16)
        inner = lax.dot_general(q.astype(BF16), k.astype(BF16), (((1,), (1,)), ((), ())),
                                preferred_element_type=F32) * decay
        outs.append(_dot(inner.astype(BF16), v))
        kv_f.append(tdot((k * kdec_f).astype(BF16), v))
        kv_b.append(tdot((k * kdec_b).astype(BF16), v))
    s_f = s0_f
    for c in range(nc):
        outs[c] = outs[c] + _dot((chunk(q_ref, c) * qdec_f).astype(BF16), s_f.astype(BF16))
        s_f = cd_f * s_f + kv_f[c]
    s_b = s0_b
    for c in reversed(range(nc)):
        outs[c] = outs[c] + _dot((chunk(q_ref, c) * qdec_b).astype(BF16), s_b.astype(BF16))
        s_b = cd_b * s_b + kv_b[c]
    for c in range(nc):
        o = outs[c]
        mu = jnp.mean(o, axis=-1, keepdims=True)
        var = jnp.mean(jnp.square(o - mu), axis=-1, keepdims=True)
        o = ((o - mu) * lax.rsqrt(var + EPS)) * gn_ref[:, cols]
        g = chunk(g_ref, c)
        o_ref[c * C:(c + 1) * C, cols] = ((g * jax.nn.sigmoid(g)) * o).astype(BF16)
    return s_f, s_b


def _ret_kernel(lg_ref, q_ref, k_ref, v_ref, g_ref, gn_ref, *rest, seq_len, heads, has_state):
    if has_state:
        s0_ref, o_ref = rest
    else:
        o_ref, sfin_ref = rest
    hg = pl.program_id(1)
    for j in range(heads):
        hd = hg * heads + j
        cols = slice(j * RET_DK, (j + 1) * RET_DK)
        if has_state:
            s0_f, s0_b = s0_ref[0, j], s0_ref[1, j]
        else:
            s0_f = s0_b = jnp.zeros((RET_DK, RET_DV), F32)
        s_f, s_b = _ret_head(lg_ref[0, hd], lg_ref[1, hd], q_ref, k_ref, v_ref, g_ref, gn_ref, o_ref,
                             s0_f, s0_b, cols, seq_len)
        if not has_state:
            sfin_ref[0, j] = s_f
            sfin_ref[1, j] = s_b


def _retention(proj, lg, gn_g, s0, n_seq, seq_len, blk0, heads):
    H = RET_HEADS
    width = heads * RET_DK

    def col_spec(off):
        first = (S5_WIDTH + off * RET_WIDTH) // width
        return pl.BlockSpec((seq_len, width), lambda b, hg: (blk0 + b, first + hg))

    state_spec = pl.BlockSpec((None, 2, heads, RET_DK, RET_DV), lambda b, hg: (b, 0, hg, 0, 0))
    in_specs = [
        pl.BlockSpec(memory_space=pltpu.SMEM),
        col_spec(0), col_spec(1), col_spec(2), col_spec(3),
        pl.BlockSpec((1, width), lambda b, hg: (0, hg)),
    ]
    args = [lg, proj, proj, proj, proj, gn_g.reshape(1, RET_WIDTH)]
    out_specs = [pl.BlockSpec((seq_len, width), lambda b, hg: (b, hg))]
    out_shape = [jax.ShapeDtypeStruct((n_seq * seq_len, RET_WIDTH), BF16)]
    if s0 is not None:
        in_specs.append(state_spec)
        args.append(s0)
    else:
        out_specs.append(state_spec)
        out_shape.append(jax.ShapeDtypeStruct((n_seq, 2, H, RET_DK, RET_DV), F32))
    return pl.pallas_call(
        functools.partial(_ret_kernel, seq_len=seq_len, heads=heads, has_state=s0 is not None),
        grid=(n_seq, H // heads),
        in_specs=in_specs,
        out_specs=out_specs,
        out_shape=out_shape,
        compiler_params=_cparams(("parallel", "parallel")),
        name="retention_dec" if s0 is not None else "retention_ctx",
    )(*args)


def _shift_rows(x, d, ridx):
    n = x.shape[0]
    if d == 0:
        return x
    rolled = pltpu.roll(x, d % n, axis=0)
    valid = (ridx >= d) if d > 0 else (ridx < n + d)
    return jnp.where(valid, rolled, 0.0)


def _pool_kernel(u_ref, w_ref, sc_ref, o_ref, *, seq_len):
    L = seq_len
    ridx = lax.broadcasted_iota(jnp.int32, (L, POOL_GC), 0)
    for gi, w in enumerate(POOL_WINDOWS):
        cols = slice(gi * POOL_GC, (gi + 1) * POOL_GC)
        x = u_ref[:, cols]
        half = w // 2
        back, ahead = x, x
        span = 1
        while span < half:
            back = back + _shift_rows(back, span, ridx)
            ahead = ahead + _shift_rows(ahead, -span, ridx)
            span *= 2
        win = _shift_rows(back, 1, ridx) + ahead
        lo = jnp.maximum(ridx - half, 0)
        hi = jnp.minimum(ridx + half - 1, L - 1)
        mean = win / (hi - lo + 1).astype(F32)
        mixed = _dot((mean - x).astype(BF16), w_ref[gi])
        o_ref[:, cols] = (mixed * sc_ref[:, cols]).astype(BF16)


def _pool(proj, w_pool, pool_scale, n_seq, seq_len, blk0):
    return pl.pallas_call(
        functools.partial(_pool_kernel, seq_len=seq_len),
        grid=(n_seq,),
        in_specs=[
            pl.BlockSpec((seq_len, POOL_WIDTH), lambda b: (blk0 + b, 0)),
            pl.BlockSpec((POOL_GROUPS, POOL_GC, POOL_GC), lambda b: (0, 0, 0)),
            pl.BlockSpec((1, POOL_WIDTH), lambda b: (0, 0)),
        ],
        out_specs=pl.BlockSpec((seq_len, POOL_WIDTH), lambda b: (b, 0)),
        out_shape=jax.ShapeDtypeStruct((n_seq * seq_len, POOL_WIDTH), BF16),
        compiler_params=_cparams(("parallel",)),
        name="pool",
    )(proj, w_pool, pool_scale.reshape(1, POOL_WIDTH))


def _softmax_pv(s, v, sink):
    m = jnp.maximum(jnp.max(s, axis=-1, keepdims=True), sink)
    p = jnp.exp(s - m)
    denom = jnp.sum(p, axis=-1, keepdims=True) + jnp.exp(sink - m)
    return _dot(p.astype(BF16), v) / denom


def _qk(q, k):
    return lax.dot_general(q, k, (((1,), (1,)), ((), ())), preferred_element_type=F32) * (ATT_HD ** -0.5)


def _ctx_att_kernel(sink_ref, q_ref, k_ref, v_ref, o_ref):
    kv = pl.program_id(1)
    k = k_ref[...].astype(BF16)
    v = v_ref[...].astype(BF16)
    for g in range(ATT_GROUP):
        cols = slice(g * ATT_HD, (g + 1) * ATT_HD)
        s = _qk(q_ref[:, cols].astype(BF16), k)
        o_ref[:, cols] = _softmax_pv(s, v, sink_ref[kv * ATT_GROUP + g]).astype(BF16)


def _ctx_attention(proj, sink):
    qw = ATT_GROUP * ATT_HD
    qb = POOL_WIDTH // qw
    kb = (POOL_WIDTH + ATT_HEADS * ATT_HD) // ATT_HD
    return pl.pallas_call(
        _ctx_att_kernel,
        grid=(BATCH, ATT_KV),
        in_specs=[
            pl.BlockSpec(memory_space=pltpu.SMEM),
            pl.BlockSpec((SEQ, qw), lambda b, kv: (b, qb + kv)),
            pl.BlockSpec((SEQ, ATT_HD), lambda b, kv: (b, kb + kv)),
            pl.BlockSpec((SEQ, ATT_HD), lambda b, kv: (b, kb + ATT_KV + kv)),
        ],
        out_specs=pl.BlockSpec((SEQ, qw), lambda b, kv: (b, kv)),
        out_shape=jax.ShapeDtypeStruct((N_CTX_TOK, ATT_HEADS * ATT_HD), BF16),
        compiler_params=_cparams(("parallel", "parallel")),
        name="attention_ctx",
    )(sink, proj, proj, proj)


def _rope(x, cos, sin):
    lane = lax.broadcasted_iota(jnp.int32, x.shape, 1)
    partner = jnp.where((lane & 63) < 32, pltpu.roll(x, 96, axis=1), pltpu.roll(x, 32, axis=1))
    return x * cos + partner * sin


def _dec_att_kernel(sink_ref, q_ref, k_ref, v_ref, kc_ref, vc_ref, cos_ref, sin_ref, o_ref, kr_ref):
    kv = pl.program_id(1)
    L, B = DEC_SEQ, ATT_BLOCK
    nb = L // B
    kr_ref[...] = _rope(k_ref[...], cos_ref[...], sin_ref[...]).astype(BF16)
    kc = kc_ref[...].astype(BF16)
    vc = vc_ref[...].astype(BF16)
    for i in range(nb):
        lo, hi = max(i - 1, 0) * B, min(i + 2, nb) * B
        rows = slice(i * B, (i + 1) * B)
        kwin = kr_ref[lo:hi, :]
        vals = jnp.concatenate([v_ref[lo:hi, :].astype(BF16), vc], axis=0)
        qpos = i * B + lax.broadcasted_iota(jnp.int32, (B, hi - lo), 0)
        kpos = lo + lax.broadcasted_iota(jnp.int32, (B, hi - lo), 1)
        band = jnp.abs(kpos - qpos) <= ATT_WIN
        for g in range(ATT_GROUP):
            cols = slice(g * ATT_HD, (g + 1) * ATT_HD)
            q = _rope(q_ref[rows, cols], cos_ref[rows, :], sin_ref[rows, :]).astype(BF16)
            s = jnp.concatenate([jnp.where(band, _qk(q, kwin), NEG), _qk(q, kc)], axis=1)
            o_ref[rows, cols] = _softmax_pv(s, vals, sink_ref[kv * ATT_GROUP + g]).astype(BF16)


def _rope_tables():
    half = ATT_HD // 2
    nfreq = half // 2
    freqs = ROPE_BASE ** (-jnp.arange(nfreq, dtype=F32) / nfreq)
    t = jnp.arange(DEC_SEQ)
    pos = jnp.stack([t // GRID_W, t % GRID_W], axis=1).astype(F32)
    ang = pos[:, :, None] * freqs[None, None, :]
    cos = jnp.repeat(jnp.cos(ang), 2, axis=1).reshape(DEC_SEQ, ATT_HD)
    sin = jnp.sin(ang)
    sin = jnp.stack([-sin, sin], axis=2).reshape(DEC_SEQ, ATT_HD)
    return cos, sin


def _dec_attention(proj, cache_k, cache_v, sink, cos, sin, o):
    qw = ATT_GROUP * ATT_HD
    qb = POOL_WIDTH // qw
    kb = (POOL_WIDTH + ATT_HEADS * ATT_HD) // ATT_HD
    blk0 = N_CTX_TOK // DEC_SEQ
    ck = cache_k.reshape(DEC_BATCH, N_ODD, PAST_LEN, ATT_KV * ATT_HD)
    cv = cache_v.reshape(DEC_BATCH, N_ODD, PAST_LEN, ATT_KV * ATT_HD)
    cache_spec = pl.BlockSpec((None, None, PAST_LEN, ATT_HD), lambda b, kv: (b, o, 0, kv))
    table_spec = pl.BlockSpec((DEC_SEQ, ATT_HD), lambda b, kv: (0, 0))
    return pl.pallas_call(
        _dec_att_kernel,
        grid=(DEC_BATCH, ATT_KV),
        in_specs=[
            pl.BlockSpec(memory_space=pltpu.SMEM),
            pl.BlockSpec((DEC_SEQ, qw), lambda b, kv: (blk0 + b, qb + kv)),
            pl.BlockSpec((DEC_SEQ, ATT_HD), lambda b, kv: (blk0 + b, kb + kv)),
            pl.BlockSpec((DEC_SEQ, ATT_HD), lambda b, kv: (blk0 + b, kb + ATT_KV + kv)),
            cache_spec, cache_spec, table_spec, table_spec,
        ],
        out_specs=pl.BlockSpec((DEC_SEQ, qw), lambda b, kv: (b, kv)),
        out_shape=jax.ShapeDtypeStruct((N_DEC_TOK, ATT_HEADS * ATT_HD), BF16),
        scratch_shapes=[pltpu.VMEM((DEC_SEQ, ATT_HD), BF16)],
        compiler_params=_cparams(("parallel", "parallel")),
        name="attention_dec",
    )(sink, proj, proj, proj, ck, cv, cos, sin)


def kernel(x_prompt, x_sample, c, state_s5_re, state_s5_im, state_ret, cache_k, cache_v, c_ctx, w_mod, b_mod, norm_g, ffn1_gate, ffn1_up, ffn1_down, ffn2_gate, ffn2_up, ffn2_down, even_w_in, even_w_out, s5_lam_re, s5_lam_im, s5_log_dt, s5_b_re, s5_b_im, s5_c_re, s5_c_im, s5_d, s5_glu_w, s5_glu_b, ret_decay_logit, ret_gn_g, odd_w_in, odd_w_out, pool_w, pool_scale, att_sink):
    x = jnp.concatenate([x_prompt.reshape(N_CTX_TOK, D_MODEL), x_sample.reshape(N_DEC_TOK, D_MODEL)], axis=0)
    c8 = jnp.concatenate([c_ctx[None, :], c, jnp.zeros((MOD_ROWS - 1 - DEC_BATCH, D_MODEL), F32)], axis=0)
    mod5 = _modulation(c8, w_mod, b_mod).reshape(DEPTH, MOD_ROWS, N_MOD, 1, D_MODEL)
    norm4 = norm_g.reshape(DEPTH, 6, 1, D_MODEL)
    cos, sin = _rope_tables()
    dec_blk = N_CTX_TOK // DEC_SEQ
    out_re, out_im, out_ret, out_k, out_v = [], [], [], [], []
    for l in range(DEPTH):
        x = _ffn(x, mod5, norm4, ffn1_gate, ffn1_up, ffn1_down, l, 0, 0, 1)
        if l % 2 == 0:
            e = l // 2
            proj = _proj_in(x, mod5, norm4, even_w_in[e].astype(BF16), l)
            ops = _s5_operators(s5_lam_re[e], s5_lam_im[e], s5_log_dt[e], s5_b_re[e], s5_b_im[e],
                                s5_c_re[e], s5_c_im[e])
            h0 = jnp.stack([state_s5_re[:, e], state_s5_im[:, e]], axis=2)
            h0 = jnp.transpose(h0, (3, 1, 2, 4, 0)).reshape(S5_GROUPS, 4 * S5_P, DEC_BATCH)
            h0 = jnp.repeat(h0, S5_DEC_CHUNKS, axis=2)
            y_a, s_re, s_im = _s5_mixer(proj, ops, h0, s5_d[e], s5_glu_w[e].astype(BF16), s5_glu_b[e])
            lg = jax.nn.log_sigmoid(ret_decay_logit[e].astype(F32))
            yb_ctx, s_ret = _retention(proj, lg, ret_gn_g[e], None, BATCH, SEQ, 0, 4)
            (yb_dec,) = _retention(proj, lg, ret_gn_g[e], state_ret[:, e], DEC_BATCH, DEC_SEQ, dec_blk, 1)
            y_b = (yb_ctx, yb_dec)
            out_re.append(s_re)
            out_im.append(s_im)
            out_ret.append(s_ret)
            w_out = even_w_out[e]
        else:
            o = l // 2
            proj = _proj_in(x, mod5, norm4, odd_w_in[o].astype(BF16), l)
            w_pool = pool_w[o].astype(BF16)
            y_a = (_pool(proj, w_pool, pool_scale[o], BATCH, SEQ, 0),
                   _pool(proj, w_pool, pool_scale[o], DEC_BATCH, DEC_SEQ, dec_blk))
            sink = att_sink[o].astype(F32)
            y_b = (_ctx_attention(proj, sink), _dec_attention(proj, cache_k, cache_v, sink, cos, sin, o))
            kv0 = POOL_WIDTH + ATT_HEADS * ATT_HD
            kv1 = kv0 + ATT_KV * ATT_HD
            out_k.append(proj[:N_CTX_TOK, kv0:kv1].reshape(BATCH, SEQ, ATT_KV, ATT_HD))
            out_v.append(proj[:N_CTX_TOK, kv1:].reshape(BATCH, SEQ, ATT_KV, ATT_HD))
            w_out = odd_w_out[o]
        x = _proj_out(x, y_a, y_b, mod5, norm4, w_out.astype(BF16), l)
        x = _ffn(x, mod5, norm4, ffn2_gate, ffn2_up, ffn2_down, l, 2, 4, 5)
    y_prompt = x[:N_CTX_TOK].reshape(BATCH, SEQ, D_MODEL)
    y_sample = x[N_CTX_TOK:].reshape(DEC_BATCH, DEC_SEQ, D_MODEL)
    return (y_prompt, y_sample, jnp.stack(out_re, axis=1), jnp.stack(out_im, axis=1),
            jnp.stack(out_ret, axis=1), jnp.stack(out_k, axis=1), jnp.stack(out_v, axis=1))
```

```python
import functools
import math

import jax
import jax.numpy as jnp
from jax import lax
from jax.experimental import pallas as pl
from jax.experimental.pallas import tpu as pltpu

F32 = jnp.float32
BF16 = jnp.bfloat16

D_MODEL = 2048
BATCH = 16
SEQ = 256
DEPTH = 4
DEC_BATCH = 2
DEC_SEQ = 1024
PAST_LEN = 256
GRID_W = 64
N_EVEN = (DEPTH + 1) // 2
N_ODD = DEPTH // 2
N_MOD = 9
FFN_HIDDEN = 5632
EPS = 1e-6
NEG = -1e30

S5_WIDTH = D_MODEL // 2
S5_GROUP = 16
S5_GROUPS = S5_WIDTH // S5_GROUP
S5_P = 64
RET_HEADS = 8
RET_DK = (D_MODEL // 2) // RET_HEADS
RET_DV = RET_DK
RET_WIDTH = RET_HEADS * RET_DV
RET_CHUNK = 128
EVEN_IN = S5_WIDTH + 4 * RET_WIDTH

POOL_WIDTH = D_MODEL // 2
POOL_WINDOWS = (2, 4, 8, 16)
POOL_GROUPS = 4
POOL_GC = POOL_WIDTH // POOL_GROUPS
ATT_HEADS = 8
ATT_KV = 2
ATT_HD = (D_MODEL // 2) // ATT_HEADS
ATT_GROUP = ATT_HEADS // ATT_KV
ATT_WIN = 128
ATT_BLOCK = 128
ODD_IN = POOL_WIDTH + (ATT_HEADS + 2 * ATT_KV) * ATT_HD
ROPE_BASE = 10000.0

N_CTX_TOK = BATCH * SEQ
N_DEC_TOK = DEC_BATCH * DEC_SEQ
N_TOK = N_CTX_TOK + N_DEC_TOK
MOD_ROWS = 8

S5_T = 16
S5_LANES = S5_T * S5_GROUP
S5_CTX_CHUNKS = SEQ // S5_T
S5_DEC_CHUNKS = DEC_SEQ // S5_T
S5_CTX_ROWS = BATCH * S5_CTX_CHUNKS
S5_DEC_ROWS = DEC_BATCH * S5_DEC_CHUNKS
S5_ROWS = S5_CTX_ROWS + S5_DEC_ROWS
S5_NPOW = 6

VMEM_LIMIT = 60 * 1024 * 1024


def _cparams(sem):
    return pltpu.CompilerParams(dimension_semantics=sem, vmem_limit_bytes=VMEM_LIMIT)


def _dot(a, b):
    return jnp.dot(a, b, preferred_element_type=F32)


def _rms(x, g):
    return (x * lax.rsqrt(jnp.mean(x * x, axis=-1, keepdims=True) + EPS)) * g


def _mod_row(i, tm):
    start = i * tm
    return (start >= N_CTX_TOK).astype(jnp.int32) + (start >= N_CTX_TOK + DEC_SEQ).astype(jnp.int32)


def _mod_kernel(c_ref, w_ref, b_ref, o_ref):
    c = c_ref[...]
    s = (c * jax.nn.sigmoid(c)).astype(BF16)
    o_ref[...] = _dot(s, w_ref[...].astype(BF16)) + b_ref[...]


def _modulation(c8, w_mod, b_mod):
    tn = 1024
    n = N_MOD * D_MODEL
    return pl.pallas_call(
        _mod_kernel,
        grid=(DEPTH, n // tn),
        in_specs=[
            pl.BlockSpec((MOD_ROWS, D_MODEL), lambda l, j: (0, 0)),
            pl.BlockSpec((None, D_MODEL, tn), lambda l, j: (l, 0, j)),
            pl.BlockSpec((None, 1, tn), lambda l, j: (l, 0, j)),
        ],
        out_specs=pl.BlockSpec((None, MOD_ROWS, tn), lambda l, j: (l, 0, j)),
        out_shape=jax.ShapeDtypeStruct((DEPTH, MOD_ROWS, n), F32),
        compiler_params=_cparams(("arbitrary", "arbitrary")),
        name="modulation",
    )(c8, w_mod, b_mod.reshape(DEPTH, 1, n))


def _mod_spec(l, j, tm):
    return pl.BlockSpec((None, None, None, 1, D_MODEL),
                        lambda i, *_: (l, _mod_row(i, tm), j, 0, 0))


def _gain_spec(l, j):
    return pl.BlockSpec((None, None, 1, D_MODEL), lambda i, *_: (l, j, 0, 0))


FFN_TM = 1024
FFN_TF = 256
SLAB_ROWS = 256


def _adaln_in(x, g_ref, sc_ref, sh_ref):
    return (_rms(x, g_ref[...]) * (1.0 + sc_ref[...]) + sh_ref[...]).astype(BF16)


def _ffn_kernel(x_ref, sh_ref, sc_ref, gate_ref, gin_ref, gout_ref, wg_ref, wu_ref, wd_ref,
                o_ref, h_ref, *, nf, tm):
    f = pl.program_id(1)
    slabs = [slice(r0, r0 + SLAB_ROWS) for r0 in range(0, tm, SLAB_ROWS)]

    def weights():
        return wg_ref[...].astype(BF16), wu_ref[...].astype(BF16), wd_ref[...].astype(BF16)

    def swiglu(h, w):
        g = _dot(h, w[0])
        u = _dot(h, w[1])
        a = ((g * jax.nn.sigmoid(g)) * u).astype(BF16)
        return _dot(a, w[2])

    @pl.when(f == 0)
    def _():
        w = weights()
        for rows in slabs:
            h = _adaln_in(x_ref[rows, :], gin_ref, sc_ref, sh_ref)
            h_ref[rows, :] = h
            o_ref[rows, :] = swiglu(h, w)

    @pl.when(jnp.logical_and(f > 0, f < nf - 1))
    def _():
        o_ref[...] += swiglu(h_ref[...], weights())

    @pl.when(f == nf - 1)
    def _():
        w = weights()
        for rows in slabs:
            y = o_ref[rows, :] + swiglu(h_ref[rows, :], w)
            o_ref[rows, :] = x_ref[rows, :] + (0.5 * gate_ref[...]) * _rms(y, gout_ref[...])


def _ffn(x, mod5, norm4, w_gate, w_up, w_down, l, j, g_in, g_out):
    tm, tf = FFN_TM, FFN_TF
    nf = FFN_HIDDEN // tf
    return pl.pallas_call(
        functools.partial(_ffn_kernel, nf=nf, tm=tm),
        grid=(N_TOK // tm, nf),
        in_specs=[
            pl.BlockSpec((tm, D_MODEL), lambda i, f: (i, 0)),
            _mod_spec(l, 3 * j, tm), _mod_spec(l, 3 * j + 1, tm), _mod_spec(l, 3 * j + 2, tm),
            _gain_spec(l, g_in), _gain_spec(l, g_out),
            pl.BlockSpec((None, D_MODEL, tf), lambda i, f: (l, 0, f)),
            pl.BlockSpec((None, D_MODEL, tf), lambda i, f: (l, 0, f)),
            pl.BlockSpec((None, tf, D_MODEL), lambda i, f: (l, f, 0)),
        ],
        out_specs=pl.BlockSpec((tm, D_MODEL), lambda i, f: (i, 0)),
        out_shape=jax.ShapeDtypeStruct((N_TOK, D_MODEL), F32),
        scratch_shapes=[pltpu.VMEM((tm, D_MODEL), BF16)],
        compiler_params=_cparams(("parallel", "arbitrary")),
        name="ffn",
    )(x, mod5, mod5, mod5, norm4, norm4, w_gate, w_up, w_down)


PROJ_TM = 1024
PROJ_TN = 1280


def _proj_in_kernel(x_ref, sh_ref, sc_ref, gin_ref, w_ref, o_ref, h_ref, *, tm):
    k = pl.program_id(1)

    @pl.when(k == 0)
    def _():
        w = w_ref[...].astype(BF16)
        for r0 in range(0, tm, SLAB_ROWS):
            rows = slice(r0, r0 + SLAB_ROWS)
            h = _adaln_in(x_ref[rows, :], gin_ref, sc_ref, sh_ref)
            h_ref[rows, :] = h
            o_ref[rows, :] = _dot(h, w)

    @pl.when(k > 0)
    def _():
        o_ref[...] = _dot(h_ref[...], w_ref[...].astype(BF16))


def _proj_in(x, mod5, norm4, w_in, l):
    tm, tn = PROJ_TM, PROJ_TN
    n = w_in.shape[-1]
    return pl.pallas_call(
        functools.partial(_proj_in_kernel, tm=tm),
        grid=(N_TOK // tm, n // tn),
        in_specs=[
            pl.BlockSpec((tm, D_MODEL), lambda i, k: (i, 0)),
            _mod_spec(l, 3, tm), _mod_spec(l, 4, tm),
            _gain_spec(l, 2),
            pl.BlockSpec((D_MODEL, tn), lambda i, k: (0, k)),
        ],
        out_specs=pl.BlockSpec((tm, tn), lambda i, k: (i, k)),
        out_shape=jax.ShapeDtypeStruct((N_TOK, n), F32),
        scratch_shapes=[pltpu.VMEM((tm, D_MODEL), BF16)],
        compiler_params=_cparams(("parallel", "arbitrary")),
        name="proj_in",
    )(x, mod5, mod5, norm4, w_in)


OUT_TM = 512


OUT_ROWS = 256


def _proj_out_kernel(x_ref, yac_ref, yad_ref, ybc_ref, ybd_ref, gate_ref, gout_ref, w_ref, o_ref, *, tm):
    def run(ya_ref, yb_ref):
        w = w_ref[...].astype(BF16)
        for r0 in range(0, tm, OUT_ROWS):
            rows = slice(r0, r0 + OUT_ROWS)
            y = _dot(jnp.concatenate([ya_ref[rows, :], yb_ref[rows, :]], axis=1), w)
            o_ref[rows, :] = x_ref[rows, :] + gate_ref[...] * _rms(y, gout_ref[...])

    is_ctx = pl.program_id(0) < N_CTX_TOK // tm
    pl.when(is_ctx)(lambda: run(yac_ref, ybc_ref))
    pl.when(jnp.logical_not(is_ctx))(lambda: run(yad_ref, ybd_ref))


def _proj_out(x, ya, yb, mod5, norm4, w_out, l):
    tm = OUT_TM
    half = D_MODEL // 2
    nct = N_CTX_TOK // tm

    def split_specs(y):
        ctx_spec = pl.BlockSpec((tm, half), lambda i: (jnp.minimum(i, nct - 1), 0))
        if isinstance(y, tuple):
            return y, (ctx_spec, pl.BlockSpec((tm, half), lambda i: (jnp.maximum(i - nct, 0), 0)))
        return (y, y), (ctx_spec, pl.BlockSpec((tm, half), lambda i: (jnp.maximum(i, nct), 0)))

    (ya_c, ya_d), (sa_c, sa_d) = split_specs(ya)
    (yb_c, yb_d), (sb_c, sb_d) = split_specs(yb)
    return pl.pallas_call(
        functools.partial(_proj_out_kernel, tm=tm),
        grid=(N_TOK // tm,),
        in_specs=[
            pl.BlockSpec((tm, D_MODEL), lambda i: (i, 0)),
            sa_c, sa_d, sb_c, sb_d,
            _mod_spec(l, 5, tm),
            _gain_spec(l, 3),
            pl.BlockSpec((D_MODEL, D_MODEL), lambda i: (0, 0), pipeline_mode=pl.Buffered(1)),
        ],
        out_specs=pl.BlockSpec((tm, D_MODEL), lambda i: (i, 0)),
        out_shape=jax.ShapeDtypeStruct((N_TOK, D_MODEL), F32),
        compiler_params=_cparams(("parallel",)),
        name="proj_out",
    )(x, ya_c, ya_d, yb_c, yb_d, mod5, norm4, w_out)


def _s5_operators(lam_re, lam_im, log_dt, b_re, b_im, c_re, c_im):
    T, P = S5_T, S5_P
    lam = lax.complex(jnp.minimum(lam_re.astype(F32), -1e-4), lam_im.astype(F32))
    ldt = lam * jnp.exp(log_dt.astype(F32))[..., None]
    a = jnp.exp(ldt)
    bbar = ((a - 1.0) / lam)[..., None] * lax.complex(b_re.astype(F32), b_im.astype(F32))
    c = jnp.swapaxes(lax.complex(c_re.astype(F32), c_im.astype(F32)), 2, 3)
    s = jnp.arange(T, dtype=F32)

    def rows(zf, zb):
        return jnp.concatenate([jnp.real(zf), jnp.imag(zf), jnp.real(zb), jnp.imag(zb)], axis=1)

    def powers(n_f, n_b):
        return rows(jnp.exp(ldt[0][:, :, None] * n_f), jnp.exp(ldt[1][:, :, None] * n_b))

    fields = [powers(T - 1.0 - s, s), powers(-1.0 - s, s - T), powers(s + 1.0, T - s),
              rows(bbar[0], bbar[1]), rows(c[0], c[1]), powers(jnp.full((1,), T, F32), jnp.full((1,), T, F32))]
    small = jnp.concatenate(fields, axis=2)
    return jnp.pad(small, ((0, 0), (0, 0), (0, 128 - small.shape[2])))


S5_F_INC, S5_F_KIN, S5_F_OUT, S5_F_B, S5_F_C, S5_F_A = (f * S5_T for f in range(6))


def _cmul_rows(x, ar, ai):
    h = x.shape[0] // 2
    xr, xi = x[:h], x[h:]
    return jnp.concatenate([xr * ar - xi * ai, xr * ai + xi * ar], axis=0)


def _s5_lane_scan(inc, a_rows, h0, lc):
    P = S5_P
    cidx = lax.broadcasted_iota(jnp.int32, (2 * P, 128), 1) & (lc - 1)
    fwd, bwd = inc[:2 * P], inc[2 * P:]
    arf, aif, arb, aib = a_rows[:P], a_rows[P:2 * P], a_rows[2 * P:3 * P], a_rows[3 * P:]
    if h0 is not None:
        fwd = fwd + jnp.where(cidx == 0, _cmul_rows(h0[:2 * P], arf, aif), 0.0)
        bwd = bwd + jnp.where(cidx == lc - 1, _cmul_rows(h0[2 * P:], arb, aib), 0.0)
    d = 1
    while d < lc:
        sh = jnp.where(cidx >= d, pltpu.roll(fwd, d, axis=1), 0.0)
        fwd = fwd + _cmul_rows(sh, arf, aif)
        sh = jnp.where(cidx < lc - d, pltpu.roll(bwd, 128 - d, axis=1), 0.0)
        bwd = bwd + _cmul_rows(sh, arb, aib)
        d *= 2
        if d < lc:
            arf, aif = arf * arf - aif * aif, 2.0 * arf * aif
            arb, aib = arb * arb - aib * aib, 2.0 * arb * aib
    first_f = 0.0 if h0 is None else h0[:2 * P]
    first_b = 0.0 if h0 is None else h0[2 * P:]
    start_f = jnp.where(cidx >= 1, pltpu.roll(fwd, 1, axis=1), first_f)
    start_b = jnp.where(cidx < lc - 1, pltpu.roll(bwd, 127, axis=1), first_b)
    return jnp.concatenate([start_f, start_b], axis=0), jnp.concatenate([fwd, bwd], axis=0)


def _dot_split(a, b):
    a_hi = a.astype(BF16)
    b_hi = b.astype(BF16)
    a_lo = (a - a_hi.astype(F32)).astype(BF16)
    b_lo = (b - b_hi.astype(F32)).astype(BF16)
    return _dot(a_hi, b_hi) + (_dot(a_hi, b_lo) + _dot(a_lo, b_hi))


S5_SLAB = 128 // S5_GROUP


def _split3(x):
    hi = x.astype(BF16)
    r = x - hi.astype(F32)
    mid = r.astype(BF16)
    return hi, mid, (r - mid.astype(F32)).astype(BF16)


def _cprod_rows(a, b, conj_sign):
    h = a.shape[0] // 2
    ar, ai, br, bi = a[:h], a[h:], b[:h], b[h:]
    return jnp.concatenate([ar * br - ai * bi, conj_sign * (ar * bi + ai * br)], axis=0)


def _s5_kernel(u_ref, f_ref, h0_ref, y_ref, fin_ref, xt_ref, yt_ref):
    T, K, P, R = S5_T, S5_GROUP, S5_P, S5_ROWS
    F = S5_LANES
    for s in range(T):
        ut = u_ref[pl.ds(s, R, stride=T), :].T
        for gg in range(S5_SLAB):
            xt_ref[gg, s * K:(s + 1) * K, :] = ut[gg * K:(gg + 1) * K, :].astype(BF16)
    shift = K.bit_length() - 1
    tok_out = lax.broadcasted_iota(jnp.int32, (F, F), 0) >> shift
    tok_in = lax.broadcasted_iota(jnp.int32, (F, F), 1) >> shift
    lane = lax.broadcasted_iota(jnp.int32, (128, F), 0)
    feat = lax.broadcasted_iota(jnp.int32, (128, F), 1)
    tok, ch = feat >> shift, feat & (K - 1)

    def spread(field, index):
        return jnp.where(lane == field + index, 1.0, 0.0).astype(BF16)

    e_inc, e_kin, e_out = spread(S5_F_INC, tok), spread(S5_F_KIN, tok), spread(S5_F_OUT, tok)
    e_b, e_c = spread(S5_F_B, ch), spread(S5_F_C, ch)
    lane_a = lax.broadcasted_iota(jnp.int32, (128, 128), 0)
    e_a = jnp.where(lane_a == S5_F_A, 1.0, 0.0).astype(BF16)
    seq = lax.broadcasted_iota(jnp.int32, (128, 128), 1)
    nseq = 128 // S5_CTX_CHUNKS
    sel_last = [jnp.where((lane_a == (seq - o) * S5_CTX_CHUNKS + S5_CTX_CHUNKS - 1) & (seq >= o) & (seq < o + nseq),
                          1.0, 0.0).astype(BF16) for o in (0, nseq)]
    sel_first = [jnp.where((lane_a == (seq - o) * S5_CTX_CHUNKS) & (seq >= o) & (seq < o + nseq),
                           1.0, 0.0).astype(BF16) for o in (0, nseq)]

    def group(gg, carry):
        hi, mid, lo = _split3(f_ref[gg])

        def expand(e, exact=False):
            out = _dot(hi, e) + _dot(mid, e)
            return out + _dot(lo, e) if exact else out

        def cprod(pw, other, conj_sign):
            return jnp.concatenate([_cprod_rows(pw[:2 * P], other[:2 * P], conj_sign),
                                    _cprod_rows(pw[2 * P:], other[2 * P:], conj_sign)], axis=0)

        bb = expand(e_b)
        inc_op = cprod(expand(e_inc), bb, 1.0)
        kin_op = cprod(expand(e_kin), bb, 1.0)
        out_op = cprod(expand(e_out), expand(e_c), -1.0).T
        a_rows = expand(e_a, exact=True)
        resp_f = _dot_split(out_op[:, :2 * P], kin_op[:2 * P])
        resp_b = _dot_split(out_op[:, 2 * P:], kin_op[2 * P:])
        resp = jnp.where(tok_out >= tok_in, resp_f, 0.0) + jnp.where(tok_out <= tok_in, resp_b, 0.0)
        ops = jnp.concatenate([resp, inc_op], axis=0).astype(BF16)
        z = _dot(ops, xt_ref[gg])
        lanes = S5_CTX_ROWS // 2
        st_a, sc_a = _s5_lane_scan(z[F:, :lanes], a_rows, None, S5_CTX_CHUNKS)
        st_b, sc_b = _s5_lane_scan(z[F:, lanes:2 * lanes], a_rows, None, S5_CTX_CHUNKS)
        st_d, _ = _s5_lane_scan(z[F:, 2 * lanes:], a_rows, h0_ref[gg], S5_DEC_CHUNKS)
        start = jnp.concatenate([st_a, st_b, st_d], axis=1).astype(BF16)
        yt_ref[gg] = z[:F] + _dot(out_op.astype(BF16), start)
        fin_f = sum(_dot(t, sel_last[0]) for t in _split3(sc_a[:2 * P])) \
            + sum(_dot(t, sel_last[1]) for t in _split3(sc_b[:2 * P]))
        fin_b = sum(_dot(t, sel_first[0]) for t in _split3(sc_a[2 * P:])) \
            + sum(_dot(t, sel_first[1]) for t in _split3(sc_b[2 * P:]))
        fin_ref[gg] = jnp.concatenate([fin_f, fin_b], axis=0)
        return carry

    lax.fori_loop(0, S5_SLAB, group, 0)
    for i in range(T):
        yt = jnp.concatenate([yt_ref[gg, i * K:(i + 1) * K, :] for gg in range(S5_SLAB)], axis=0)
        y_ref[pl.ds(i, R, stride=T), :] = yt.T


def _s5_scan(proj, factors, h0):
    G, F, P4 = S5_GROUPS, S5_LANES, 4 * S5_P
    assert S5_CTX_ROWS == 256 and S5_DEC_ROWS == 128 and F == 256 and P4 == 256 and S5_T == S5_GROUP
    op_spec = pl.BlockSpec((S5_SLAB, P4, 128), lambda j: (j, 0, 0))
    return pl.pallas_call(
        _s5_kernel,
        grid=(G // S5_SLAB,),
        in_specs=[pl.BlockSpec((N_TOK, 128), lambda j: (0, j)), op_spec, op_spec],
        out_specs=[pl.BlockSpec((N_TOK, 128), lambda j: (0, j)), op_spec],
        out_shape=[
            jax.ShapeDtypeStruct((N_TOK, S5_WIDTH), F32),
            jax.ShapeDtypeStruct((G, P4, 128), F32),
        ],
        scratch_shapes=[pltpu.VMEM((S5_SLAB, F, S5_ROWS), BF16), pltpu.VMEM((S5_SLAB, F, S5_ROWS), F32)],
        compiler_params=_cparams(("parallel",)),
        name="s5_scan",
    )(proj, factors, h0)


S5_POST_TM = 512


def _s5_post_kernel(y_ref, u_ref, d_ref, w_ref, b_ref, o_ref):
    y = y_ref[...] + d_ref[...] * u_ref[...]
    z = 0.5 * y * (1.0 + jnp.tanh(math.sqrt(2.0 / math.pi) * (y + 0.044715 * (y * y * y))))
    gate = jax.nn.sigmoid(_dot(z.astype(BF16), w_ref[...].astype(BF16)) + b_ref[...])
    o_ref[...] = (z * gate).astype(BF16)


def _s5_post(y, proj, d_skip, w_glu, b_glu):
    tm = S5_POST_TM
    return pl.pallas_call(
        _s5_post_kernel,
        grid=(N_TOK // tm,),
        in_specs=[
            pl.BlockSpec((tm, S5_WIDTH), lambda i: (i, 0)),
            pl.BlockSpec((tm, S5_WIDTH), lambda i: (i, 0)),
            pl.BlockSpec((1, S5_WIDTH), lambda i: (0, 0)),
            pl.BlockSpec((S5_WIDTH, S5_WIDTH), lambda i: (0, 0), pipeline_mode=pl.Buffered(1)),
            pl.BlockSpec((1, S5_WIDTH), lambda i: (0, 0)),
        ],
        out_specs=pl.BlockSpec((tm, S5_WIDTH), lambda i: (i, 0)),
        out_shape=jax.ShapeDtypeStruct((N_TOK, S5_WIDTH), BF16),
        compiler_params=_cparams(("parallel",)),
        name="s5_post",
    )(y, proj, d_skip.reshape(1, S5_WIDTH), w_glu, b_glu.reshape(1, S5_WIDTH))


def _s5_mixer(proj, ops, h0, d_skip, w_glu, b_glu):
    y, fin = _s5_scan(proj, ops, h0)
    y_a = _s5_post(y, proj, d_skip, w_glu, b_glu)
    fin = fin[:, :, :BATCH].reshape(S5_GROUPS, 2, 2, S5_P, BATCH)
    fin = jnp.transpose(fin, (4, 1, 2, 0, 3))
    return y_a, fin[:, :, 0], fin[:, :, 1]


def _ret_head(lgf, lgb, q_ref, k_ref, v_ref, g_ref, gn_ref, o_ref, s0_f, s0_b, cols, seq_len):
    C = RET_CHUNK
    nc = seq_len // C
    row = lax.broadcasted_iota(jnp.int32, (C, C), 0).astype(F32)
    col = lax.broadcasted_iota(jnp.int32, (C, C), 1).astype(F32)
    rel = row - col
    decay = (jnp.where(rel >= 0, jnp.exp(jnp.maximum(rel, 0.0) * lgf), 0.0)
             + jnp.where(rel <= 0, jnp.exp(jnp.maximum(-rel, 0.0) * lgb), 0.0))
    kdec_f = jnp.exp((C - 1.0 - row) * lgf)
    kdec_b = jnp.exp(row * lgb)
    qdec_f = jnp.exp((row + 1.0) * lgf)
    qdec_b = jnp.exp((C - row) * lgb)
    cd_f = jnp.exp(jnp.full((1, RET_DV), C * lgf, F32))
    cd_b = jnp.exp(jnp.full((1, RET_DV), C * lgb, F32))

    def chunk(ref, c):
        return ref[c * C:(c + 1) * C, cols]

    tdot = functools.partial(lax.dot_general, dimension_numbers=(((0,), (0,)), ((), ())),
                             preferred_element_type=F32)
    outs = []
    kv_f, kv_b = [], []
    for c in range(nc):
        q = chunk(q_ref, c)
        k = chunk(k_ref, c) * (RET_DK ** -0.5)
        v = chunk(v_ref, c).astype(BF16)
        inner = lax.dot_general(q.astype(BF16), k.astype(BF16), (((1,), (1,)), ((), ())),
                                preferred_element_type=F32) * decay
        outs.append(_dot(inner.astype(BF16), v))
        kv_f.append(tdot((k * kdec_f).astype(BF16), v))
        kv_b.append(tdot((k * kdec_b).astype(BF16), v))
    s_f = s0_f
    for c in range(nc):
        outs[c] = outs[c] + _dot((chunk(q_ref, c) * qdec_f).astype(BF16), s_f.astype(BF16))
        s_f = cd_f * s_f + kv_f[c]
    s_b = s0_b
    for c in reversed(range(nc)):
        outs[c] = outs[c] + _dot((chunk(q_ref, c) * qdec_b).astype(BF16), s_b.astype(BF16))
        s_b = cd_b * s_b + kv_b[c]
    for c in range(nc):
        o = outs[c]
        mu = jnp.mean(o, axis=-1, keepdims=True)
        var = jnp.mean(jnp.square(o - mu), axis=-1, keepdims=True)
        o = ((o - mu) * lax.rsqrt(var + EPS)) * gn_ref[:, cols]
        g = chunk(g_ref, c)
        o_ref[c * C:(c + 1) * C, cols] = ((g * jax.nn.sigmoid(g)) * o).astype(BF16)
    return s_f, s_b


def _ret_kernel(lg_ref, q_ref, k_ref, v_ref, g_ref, gn_ref, *rest, seq_len, heads, has_state):
    if has_state:
        s0_ref, o_ref = rest
    else:
        o_ref, sfin_ref = rest
    hg = pl.program_id(1)
    for j in range(heads):
        hd = hg * heads + j
        cols = slice(j * RET_DK, (j + 1) * RET_DK)
        if has_state:
            s0_f, s0_b = s0_ref[0, j], s0_ref[1, j]
        else:
            s0_f = s0_b = jnp.zeros((RET_DK, RET_DV), F32)
        s_f, s_b = _ret_head(lg_ref[0, hd], lg_ref[1, hd], q_ref, k_ref, v_ref, g_ref, gn_ref, o_ref,
                             s0_f, s0_b, cols, seq_len)
        if not has_state:
            sfin_ref[0, j] = s_f
            sfin_ref[1, j] = s_b


def _retention(proj, lg, gn_g, s0, n_seq, seq_len, blk0, heads):
    H = RET_HEADS
    width = heads * RET_DK

    def col_spec(off):
        first = (S5_WIDTH + off * RET_WIDTH) // width
        return pl.BlockSpec((seq_len, width), lambda b, hg: (blk0 + b, first + hg))

    state_spec = pl.BlockSpec((None, 2, heads, RET_DK, RET_DV), lambda b, hg: (b, 0, hg, 0, 0))
    in_specs = [
        pl.BlockSpec(memory_space=pltpu.SMEM),
        col_spec(0), col_spec(1), col_spec(2), col_spec(3),
        pl.BlockSpec((1, width), lambda b, hg: (0, hg)),
    ]
    args = [lg, proj, proj, proj, proj, gn_g.reshape(1, RET_WIDTH)]
    out_specs = [pl.BlockSpec((seq_len, width), lambda b, hg: (b, hg))]
    out_shape = [jax.ShapeDtypeStruct((n_seq * seq_len, RET_WIDTH), BF16)]
    if s0 is not None:
        in_specs.append(state_spec)
        args.append(s0)
    else:
        out_specs.append(state_spec)
        out_shape.append(jax.ShapeDtypeStruct((n_seq, 2, H, RET_DK, RET_DV), F32))
    return pl.pallas_call(
        functools.partial(_ret_kernel, seq_len=seq_len, heads=heads, has_state=s0 is not None),
        grid=(n_seq, H // heads),
        in_specs=in_specs,
        out_specs=out_specs,
        out_shape=out_shape,
        compiler_params=_cparams(("parallel", "parallel")),
        name="retention_dec" if s0 is not None else "retention_ctx",
    )(*args)


def _shift_rows(x, d, ridx):
    n = x.shape[0]
    if d == 0:
        return x
    rolled = pltpu.roll(x, d % n, axis=0)
    valid = (ridx >= d) if d > 0 else (ridx < n + d)
    return jnp.where(valid, rolled, 0.0)


def _pool_kernel(u_ref, w_ref, sc_ref, o_ref, *, seq_len):
    L = seq_len
    ridx = lax.broadcasted_iota(jnp.int32, (L, POOL_GC), 0)
    for gi, w in enumerate(POOL_WINDOWS):
        cols = slice(gi * POOL_GC, (gi + 1) * POOL_GC)
        x = u_ref[:, cols]
        half = w // 2
        back, ahead = x, x
        span = 1
        while span < half:
            back = back + _shift_rows(back, span, ridx)
            ahead = ahead + _shift_rows(ahead, -span, ridx)
            span *= 2
        win = _shift_rows(back, 1, ridx) + ahead
        lo = jnp.maximum(ridx - half, 0)
        hi = jnp.minimum(ridx + half - 1, L - 1)
        mean = win / (hi - lo + 1).astype(F32)
        mixed = _dot((mean - x).astype(BF16), w_ref[gi].astype(BF16))
        o_ref[:, cols] = (mixed * sc_ref[:, cols]).astype(BF16)


def _pool(proj, w_pool, pool_scale, n_seq, seq_len, blk0):
    return pl.pallas_call(
        functools.partial(_pool_kernel, seq_len=seq_len),
        grid=(n_seq,),
        in_specs=[
            pl.BlockSpec((seq_len, POOL_WIDTH), lambda b: (blk0 + b, 0)),
            pl.BlockSpec((POOL_GROUPS, POOL_GC, POOL_GC), lambda b: (0, 0, 0)),
            pl.BlockSpec((1, POOL_WIDTH), lambda b: (0, 0)),
        ],
        out_specs=pl.BlockSpec((seq_len, POOL_WIDTH), lambda b: (b, 0)),
        out_shape=jax.ShapeDtypeStruct((n_seq * seq_len, POOL_WIDTH), BF16),
        compiler_params=_cparams(("parallel",)),
        name="pool",
    )(proj, w_pool, pool_scale.reshape(1, POOL_WIDTH))


def _softmax_pv(s, v, sink):
    m = jnp.maximum(jnp.max(s, axis=-1, keepdims=True), sink)
    p = jnp.exp(s - m)
    denom = jnp.sum(p, axis=-1, keepdims=True) + jnp.exp(sink - m)
    return _dot(p.astype(BF16), v) / denom


def _qk(q, k):
    return lax.dot_general(q, k, (((1,), (1,)), ((), ())), preferred_element_type=F32) * (ATT_HD ** -0.5)


def _ctx_att_kernel(sink_ref, q_ref, k_ref, v_ref, o_ref, ko_ref, vo_ref):
    kv = pl.program_id(1)
    ko_ref[...] = k_ref[...]
    vo_ref[...] = v_ref[...]
    k = k_ref[...].astype(BF16)
    v = v_ref[...].astype(BF16)
    for g in range(ATT_GROUP):
        cols = slice(g * ATT_HD, (g + 1) * ATT_HD)
        s = _qk(q_ref[:, cols].astype(BF16), k)
        o_ref[:, cols] = _softmax_pv(s, v, sink_ref[kv * ATT_GROUP + g]).astype(BF16)


def _ctx_attention(proj, sink):
    qw = ATT_GROUP * ATT_HD
    qb = POOL_WIDTH // qw
    kb = (POOL_WIDTH + ATT_HEADS * ATT_HD) // ATT_HD
    return pl.pallas_call(
        _ctx_att_kernel,
        grid=(BATCH, ATT_KV),
        in_specs=[
            pl.BlockSpec(memory_space=pltpu.SMEM),
            pl.BlockSpec((SEQ, qw), lambda b, kv: (b, qb + kv)),
            pl.BlockSpec((SEQ, ATT_HD), lambda b, kv: (b, kb + kv)),
            pl.BlockSpec((SEQ, ATT_HD), lambda b, kv: (b, kb + ATT_KV + kv)),
        ],
        out_specs=[pl.BlockSpec((SEQ, qw), lambda b, kv: (b, kv)),
                   pl.BlockSpec((SEQ, ATT_HD), lambda b, kv: (b, kv)),
                   pl.BlockSpec((SEQ, ATT_HD), lambda b, kv: (b, kv))],
        out_shape=[jax.ShapeDtypeStruct((N_CTX_TOK, ATT_HEADS * ATT_HD), BF16),
                   jax.ShapeDtypeStruct((N_CTX_TOK, ATT_KV * ATT_HD), F32),
                   jax.ShapeDtypeStruct((N_CTX_TOK, ATT_KV * ATT_HD), F32)],
        compiler_params=_cparams(("parallel", "parallel")),
        name="attention_ctx",
    )(sink, proj, proj, proj)


def _rope(x, cos, sin):
    lane = lax.broadcasted_iota(jnp.int32, x.shape, 1)
    partner = jnp.where((lane & 63) < 32, pltpu.roll(x, 96, axis=1), pltpu.roll(x, 32, axis=1))
    return x * cos + partner * sin


def _dec_att_kernel(sink_ref, q_ref, k_ref, v_ref, kc_ref, vc_ref, cos_ref, sin_ref, o_ref, kr_ref):
    kv = pl.program_id(1)
    L, B = DEC_SEQ, ATT_BLOCK
    nb = L // B
    kr_ref[...] = _rope(k_ref[...], cos_ref[...], sin_ref[...]).astype(BF16)
    kc = kc_ref[...].astype(BF16)
    vc = vc_ref[...].astype(BF16)
    for i in range(nb):
        lo, hi = max(i - 1, 0) * B, min(i + 2, nb) * B
        rows = slice(i * B, (i + 1) * B)
        kwin = kr_ref[lo:hi, :]
        vals = jnp.concatenate([v_ref[lo:hi, :].astype(BF16), vc], axis=0)
        qpos = i * B + lax.broadcasted_iota(jnp.int32, (B, hi - lo), 0)
        kpos = lo + lax.broadcasted_iota(jnp.int32, (B, hi - lo), 1)
        band = jnp.abs(kpos - qpos) <= ATT_WIN
        for g in range(ATT_GROUP):
            cols = slice(g * ATT_HD, (g + 1) * ATT_HD)
            q = _rope(q_ref[rows, cols], cos_ref[rows, :], sin_ref[rows, :]).astype(BF16)
            s = jnp.concatenate([jnp.where(band, _qk(q, kwin), NEG), _qk(q, kc)], axis=1)
            o_ref[rows, cols] = _softmax_pv(s, vals, sink_ref[kv * ATT_GROUP + g]).astype(BF16)


def _rope_tables():
    half = ATT_HD // 2
    nfreq = half // 2
    freqs = ROPE_BASE ** (-jnp.arange(nfreq, dtype=F32) / nfreq)
    t = jnp.arange(DEC_SEQ)
    pos = jnp.stack([t // GRID_W, t % GRID_W], axis=1).astype(F32)
    ang = pos[:, :, None] * freqs[None, None, :]
    cos = jnp.repeat(jnp.cos(ang), 2, axis=1).reshape(DEC_SEQ, ATT_HD)
    sin = jnp.sin(ang)
    sin = jnp.stack([-sin, sin], axis=2).reshape(DEC_SEQ, ATT_HD)
    return cos, sin


def _dec_attention(proj, cache_k, cache_v, sink, cos, sin, o):
    qw = ATT_GROUP * ATT_HD
    qb = POOL_WIDTH // qw
    kb = (POOL_WIDTH + ATT_HEADS * ATT_HD) // ATT_HD
    blk0 = N_CTX_TOK // DEC_SEQ
    ck = cache_k.reshape(DEC_BATCH, N_ODD, PAST_LEN, ATT_KV * ATT_HD)
    cv = cache_v.reshape(DEC_BATCH, N_ODD, PAST_LEN, ATT_KV * ATT_HD)
    cache_spec = pl.BlockSpec((None, None, PAST_LEN, ATT_HD), lambda b, kv: (b, o, 0, kv))
    table_spec = pl.BlockSpec((DEC_SEQ, ATT_HD), lambda b, kv: (0, 0))
    return pl.pallas_call(
        _dec_att_kernel,
        grid=(DEC_BATCH, ATT_KV),
        in_specs=[
            pl.BlockSpec(memory_space=pltpu.SMEM),
            pl.BlockSpec((DEC_SEQ, qw), lambda b, kv: (blk0 + b, qb + kv)),
            pl.BlockSpec((DEC_SEQ, ATT_HD), lambda b, kv: (blk0 + b, kb + kv)),
            pl.BlockSpec((DEC_SEQ, ATT_HD), lambda b, kv: (blk0 + b, kb + ATT_KV + kv)),
            cache_spec, cache_spec, table_spec, table_spec,
        ],
        out_specs=pl.BlockSpec((DEC_SEQ, qw), lambda b, kv: (b, kv)),
        out_shape=jax.ShapeDtypeStruct((N_DEC_TOK, ATT_HEADS * ATT_HD), BF16),
        scratch_shapes=[pltpu.VMEM((DEC_SEQ, ATT_HD), BF16)],
        compiler_params=_cparams(("parallel", "parallel")),
        name="attention_dec",
    )(sink, proj, proj, proj, ck, cv, cos, sin)


def kernel(x_prompt, x_sample, c, state_s5_re, state_s5_im, state_ret, cache_k, cache_v, c_ctx, w_mod, b_mod, norm_g, ffn1_gate, ffn1_up, ffn1_down, ffn2_gate, ffn2_up, ffn2_down, even_w_in, even_w_out, s5_lam_re, s5_lam_im, s5_log_dt, s5_b_re, s5_b_im, s5_c_re, s5_c_im, s5_d, s5_glu_w, s5_glu_b, ret_decay_logit, ret_gn_g, odd_w_in, odd_w_out, pool_w, pool_scale, att_sink):
    x = jnp.concatenate([x_prompt.reshape(N_CTX_TOK, D_MODEL), x_sample.reshape(N_DEC_TOK, D_MODEL)], axis=0)
    c8 = jnp.concatenate([c_ctx[None, :], c, jnp.zeros((MOD_ROWS - 1 - DEC_BATCH, D_MODEL), F32)], axis=0)
    mod5 = _modulation(c8, w_mod, b_mod).reshape(DEPTH, MOD_ROWS, N_MOD, 1, D_MODEL)
    norm4 = norm_g.reshape(DEPTH, 6, 1, D_MODEL)
    cos, sin = _rope_tables()
    dec_blk = N_CTX_TOK // DEC_SEQ
    out_re, out_im, out_ret, out_k, out_v = [], [], [], [], []
    for l in range(DEPTH):
        x = _ffn(x, mod5, norm4, ffn1_gate, ffn1_up, ffn1_down, l, 0, 0, 1)
        if l % 2 == 0:
            e = l // 2
            proj = _proj_in(x, mod5, norm4, even_w_in[e], l)
            ops = _s5_operators(s5_lam_re[e], s5_lam_im[e], s5_log_dt[e], s5_b_re[e], s5_b_im[e],
                                s5_c_re[e], s5_c_im[e])
            h0 = jnp.stack([state_s5_re[:, e], state_s5_im[:, e]], axis=2)
            h0 = jnp.transpose(h0, (3, 1, 2, 4, 0)).reshape(S5_GROUPS, 4 * S5_P, DEC_BATCH)
            h0 = jnp.repeat(h0, S5_DEC_CHUNKS, axis=2)
            y_a, s_re, s_im = _s5_mixer(proj, ops, h0, s5_d[e], s5_glu_w[e], s5_glu_b[e])
            lg = jax.nn.log_sigmoid(ret_decay_logit[e].astype(F32))
            yb_ctx, s_ret = _retention(proj, lg, ret_gn_g[e], None, BATCH, SEQ, 0, 4)
            (yb_dec,) = _retention(proj, lg, ret_gn_g[e], state_ret[:, e], DEC_BATCH, DEC_SEQ, dec_blk, 1)
            y_b = (yb_ctx, yb_dec)
            out_re.append(s_re)
            out_im.append(s_im)
            out_ret.append(s_ret)
            w_out = even_w_out[e]
        else:
            o = l // 2
            proj = _proj_in(x, mod5, norm4, odd_w_in[o], l)
            w_pool = pool_w[o]
            y_a = (_pool(proj, w_pool, pool_scale[o], BATCH, SEQ, 0),
                   _pool(proj, w_pool, pool_scale[o], DEC_BATCH, DEC_SEQ, dec_blk))
            sink = att_sink[o].astype(F32)
            yd_ctx, k_ctx, v_ctx = _ctx_attention(proj, sink)
            y_b = (yd_ctx, _dec_attention(proj, cache_k, cache_v, sink, cos, sin, o))
            out_k.append(k_ctx.reshape(BATCH, SEQ, ATT_KV, ATT_HD))
            out_v.append(v_ctx.reshape(BATCH, SEQ, ATT_KV, ATT_HD))
            w_out = odd_w_out[o]
        x = _proj_out(x, y_a, y_b, mod5, norm4, w_out, l)
        x = _ffn(x, mod5, norm4, ffn2_gate, ffn2_up, ffn2_down, l, 2, 4, 5)
    y_prompt = x[:N_CTX_TOK].reshape(BATCH, SEQ, D_MODEL)
    y_sample = x[N_CTX_TOK:].reshape(DEC_BATCH, DEC_SEQ, D_MODEL)
    return (y_prompt, y_sample, jnp.stack(out_re, axis=1), jnp.stack(out_im, axis=1),
            jnp.stack(out_ret, axis=1), jnp.stack(out_k, axis=1), jnp.stack(out_v, axis=1))
```

```python
import functools
import math

import jax
import jax.numpy as jnp
from jax import lax
from jax.experimental import pallas as pl
from jax.experimental.pallas import tpu as pltpu

F32 = jnp.float32
BF16 = jnp.bfloat16

D_MODEL = 2048
BATCH = 16
SEQ = 256
DEPTH = 4
DEC_BATCH = 2
DEC_SEQ = 1024
PAST_LEN = 256
GRID_W = 64
N_EVEN = (DEPTH + 1) // 2
N_ODD = DEPTH // 2
N_MOD = 9
FFN_HIDDEN = 5632
EPS = 1e-6
NEG = -1e30

S5_WIDTH = D_MODEL // 2
S5_GROUP = 16
S5_GROUPS = S5_WIDTH // S5_GROUP
S5_P = 64
RET_HEADS = 8
RET_DK = (D_MODEL // 2) // RET_HEADS
RET_DV = RET_DK
RET_WIDTH = RET_HEADS * RET_DV
RET_CHUNK = 128
EVEN_IN = S5_WIDTH + 4 * RET_WIDTH

POOL_WIDTH = D_MODEL // 2
POOL_WINDOWS = (2, 4, 8, 16)
POOL_GROUPS = 4
POOL_GC = POOL_WIDTH // POOL_GROUPS
ATT_HEADS = 8
ATT_KV = 2
ATT_HD = (D_MODEL // 2) // ATT_HEADS
ATT_GROUP = ATT_HEADS // ATT_KV
ATT_WIN = 128
ATT_BLOCK = 128
ODD_IN = POOL_WIDTH + (ATT_HEADS + 2 * ATT_KV) * ATT_HD
ROPE_BASE = 10000.0

N_CTX_TOK = BATCH * SEQ
N_DEC_TOK = DEC_BATCH * DEC_SEQ
N_TOK = N_CTX_TOK + N_DEC_TOK
MOD_ROWS = 8

S5_T = 16
S5_LANES = S5_T * S5_GROUP
S5_CTX_CHUNKS = SEQ // S5_T
S5_DEC_CHUNKS = DEC_SEQ // S5_T
S5_CTX_ROWS = BATCH * S5_CTX_CHUNKS
S5_DEC_ROWS = DEC_BATCH * S5_DEC_CHUNKS
S5_ROWS = S5_CTX_ROWS + S5_DEC_ROWS
S5_NPOW = 6

VMEM_LIMIT = 60 * 1024 * 1024


def _cparams(sem):
    return pltpu.CompilerParams(dimension_semantics=sem, vmem_limit_bytes=VMEM_LIMIT)


def _dot(a, b):
    return jnp.dot(a, b, preferred_element_type=F32)


def _rms(x, g):
    return (x * lax.rsqrt(jnp.mean(x * x, axis=-1, keepdims=True) + EPS)) * g


def _mod_row(i, tm):
    start = i * tm
    return (start >= N_CTX_TOK).astype(jnp.int32) + (start >= N_CTX_TOK + DEC_SEQ).astype(jnp.int32)


def _mod_kernel(c_ref, w_ref, b_ref, o_ref):
    c = c_ref[...]
    s = (c * jax.nn.sigmoid(c)).astype(BF16)
    o_ref[...] = _dot(s, w_ref[...].astype(BF16)) + b_ref[...]


def _modulation(c8, w_mod, b_mod):
    tn = 1024
    n = N_MOD * D_MODEL
    return pl.pallas_call(
        _mod_kernel,
        grid=(DEPTH, n // tn),
        in_specs=[
            pl.BlockSpec((MOD_ROWS, D_MODEL), lambda l, j: (0, 0)),
            pl.BlockSpec((None, D_MODEL, tn), lambda l, j: (l, 0, j)),
            pl.BlockSpec((None, 1, tn), lambda l, j: (l, 0, j)),
        ],
        out_specs=pl.BlockSpec((None, MOD_ROWS, tn), lambda l, j: (l, 0, j)),
        out_shape=jax.ShapeDtypeStruct((DEPTH, MOD_ROWS, n), F32),
        compiler_params=_cparams(("arbitrary", "arbitrary")),
        name="modulation",
    )(c8, w_mod, b_mod.reshape(DEPTH, 1, n))


def _mod_spec(l, j, tm):
    return pl.BlockSpec((None, None, None, 1, D_MODEL),
                        lambda i, *_: (l, _mod_row(i, tm), j, 0, 0))


def _gain_spec(l, j):
    return pl.BlockSpec((None, None, 1, D_MODEL), lambda i, *_: (l, j, 0, 0))


FFN_TM = 1024
FFN_TF = 256
SLAB_ROWS = 256


def _adaln_in(x, g_ref, sc_ref, sh_ref):
    return (_rms(x, g_ref[...]) * (1.0 + sc_ref[...]) + sh_ref[...]).astype(BF16)


def _ffn_kernel(x_ref, sh_ref, sc_ref, gate_ref, gin_ref, gout_ref, wg_ref, wu_ref, wd_ref,
                o_ref, h_ref, *, nf, tm):
    f = pl.program_id(1)
    slabs = [slice(r0, r0 + SLAB_ROWS) for r0 in range(0, tm, SLAB_ROWS)]

    def weights():
        return wg_ref[...].astype(BF16), wu_ref[...].astype(BF16), wd_ref[...].astype(BF16)

    def swiglu(h, w):
        g = _dot(h, w[0])
        u = _dot(h, w[1])
        a = ((g * jax.nn.sigmoid(g)) * u).astype(BF16)
        return _dot(a, w[2])

    @pl.when(f == 0)
    def _():
        w = weights()
        for rows in slabs:
            h = _adaln_in(x_ref[rows, :], gin_ref, sc_ref, sh_ref)
            h_ref[rows, :] = h
            o_ref[rows, :] = swiglu(h, w)

    @pl.when(jnp.logical_and(f > 0, f < nf - 1))
    def _():
        o_ref[...] += swiglu(h_ref[...], weights())

    @pl.when(f == nf - 1)
    def _():
        w = weights()
        for rows in slabs:
            y = o_ref[rows, :] + swiglu(h_ref[rows, :], w)
            o_ref[rows, :] = x_ref[rows, :] + (0.5 * gate_ref[...]) * _rms(y, gout_ref[...])


def _ffn(x, mod5, norm4, w_gate, w_up, w_down, l, j, g_in, g_out):
    tm, tf = FFN_TM, FFN_TF
    nf = FFN_HIDDEN // tf
    return pl.pallas_call(
        functools.partial(_ffn_kernel, nf=nf, tm=tm),
        grid=(N_TOK // tm, nf),
        in_specs=[
            pl.BlockSpec((tm, D_MODEL), lambda i, f: (i, 0)),
            _mod_spec(l, 3 * j, tm), _mod_spec(l, 3 * j + 1, tm), _mod_spec(l, 3 * j + 2, tm),
            _gain_spec(l, g_in), _gain_spec(l, g_out),
            pl.BlockSpec((None, D_MODEL, tf), lambda i, f: (l, 0, f)),
            pl.BlockSpec((None, D_MODEL, tf), lambda i, f: (l, 0, f)),
            pl.BlockSpec((None, tf, D_MODEL), lambda i, f: (l, f, 0)),
        ],
        out_specs=pl.BlockSpec((tm, D_MODEL), lambda i, f: (i, 0)),
        out_shape=jax.ShapeDtypeStruct((N_TOK, D_MODEL), F32),
        scratch_shapes=[pltpu.VMEM((tm, D_MODEL), BF16)],
        compiler_params=_cparams(("parallel", "arbitrary")),
        name="ffn",
    )(x, mod5, mod5, mod5, norm4, norm4, w_gate, w_up, w_down)


PROJ_TM = 1024
PROJ_TN = 1280


def _proj_in_kernel(x_ref, sh_ref, sc_ref, gin_ref, w_ref, o_ref, h_ref, *, tm):
    k = pl.program_id(1)

    @pl.when(k == 0)
    def _():
        w = w_ref[...].astype(BF16)
        for r0 in range(0, tm, SLAB_ROWS):
            rows = slice(r0, r0 + SLAB_ROWS)
            h = _adaln_in(x_ref[rows, :], gin_ref, sc_ref, sh_ref)
            h_ref[rows, :] = h
            o_ref[rows, :] = _dot(h, w)

    @pl.when(k > 0)
    def _():
        o_ref[...] = _dot(h_ref[...], w_ref[...].astype(BF16))


def _proj_in(x, mod5, norm4, w_in, e, l):
    tm, tn = PROJ_TM, PROJ_TN
    n = w_in.shape[-1]
    return pl.pallas_call(
        functools.partial(_proj_in_kernel, tm=tm),
        grid=(N_TOK // tm, n // tn),
        in_specs=[
            pl.BlockSpec((tm, D_MODEL), lambda i, k: (i, 0)),
            _mod_spec(l, 3, tm), _mod_spec(l, 4, tm),
            _gain_spec(l, 2),
            pl.BlockSpec((None, D_MODEL, tn), lambda i, k: (e, 0, k)),
        ],
        out_specs=pl.BlockSpec((tm, tn), lambda i, k: (i, k)),
        out_shape=jax.ShapeDtypeStruct((N_TOK, n), F32),
        scratch_shapes=[pltpu.VMEM((tm, D_MODEL), BF16)],
        compiler_params=_cparams(("parallel", "arbitrary")),
        name="proj_in",
    )(x, mod5, mod5, norm4, w_in)


OUT_TM = 512


OUT_ROWS = 256


def _proj_out_kernel(x_ref, yac_ref, yad_ref, ybc_ref, ybd_ref, gate_ref, gout_ref, w_ref, o_ref, *, tm):
    def run(ya_ref, yb_ref):
        w = w_ref[...].astype(BF16)
        for r0 in range(0, tm, OUT_ROWS):
            rows = slice(r0, r0 + OUT_ROWS)
            y = _dot(jnp.concatenate([ya_ref[rows, :], yb_ref[rows, :]], axis=1), w)
            o_ref[rows, :] = x_ref[rows, :] + gate_ref[...] * _rms(y, gout_ref[...])

    is_ctx = pl.program_id(0) < N_CTX_TOK // tm
    pl.when(is_ctx)(lambda: run(yac_ref, ybc_ref))
    pl.when(jnp.logical_not(is_ctx))(lambda: run(yad_ref, ybd_ref))


def _proj_out(x, ya, yb, mod5, norm4, w_out, e, l):
    tm = OUT_TM
    half = D_MODEL // 2
    nct = N_CTX_TOK // tm

    def split_specs(y):
        ctx_spec = pl.BlockSpec((tm, half), lambda i: (jnp.minimum(i, nct - 1), 0))
        if isinstance(y, tuple):
            return y, (ctx_spec, pl.BlockSpec((tm, half), lambda i: (jnp.maximum(i - nct, 0), 0)))
        return (y, y), (ctx_spec, pl.BlockSpec((tm, half), lambda i: (jnp.maximum(i, nct), 0)))

    (ya_c, ya_d), (sa_c, sa_d) = split_specs(ya)
    (yb_c, yb_d), (sb_c, sb_d) = split_specs(yb)
    return pl.pallas_call(
        functools.partial(_proj_out_kernel, tm=tm),
        grid=(N_TOK // tm,),
        in_specs=[
            pl.BlockSpec((tm, D_MODEL), lambda i: (i, 0)),
            sa_c, sa_d, sb_c, sb_d,
            _mod_spec(l, 5, tm),
            _gain_spec(l, 3),
            pl.BlockSpec((None, D_MODEL, D_MODEL), lambda i: (e, 0, 0), pipeline_mode=pl.Buffered(1)),
        ],
        out_specs=pl.BlockSpec((tm, D_MODEL), lambda i: (i, 0)),
        out_shape=jax.ShapeDtypeStruct((N_TOK, D_MODEL), F32),
        compiler_params=_cparams(("parallel",)),
        name="proj_out",
    )(x, ya_c, ya_d, yb_c, yb_d, mod5, norm4, w_out)


def _s5_operators(lam_re, lam_im, log_dt, b_re, b_im, c_re, c_im):
    T, P, G = S5_T, S5_P, S5_GROUPS
    lam = lax.complex(jnp.minimum(lam_re.astype(F32), -1e-4), lam_im.astype(F32))
    ldt = lam * jnp.exp(log_dt.astype(F32))[..., None]
    a = jnp.exp(ldt)
    bbar = ((a - 1.0) / lam)[..., None] * lax.complex(b_re.astype(F32), b_im.astype(F32))
    c = jnp.swapaxes(lax.complex(c_re.astype(F32), c_im.astype(F32)), 2, 3)
    s = jnp.arange(T, dtype=F32)
    full = jnp.full((1,), T, F32)
    n_f = jnp.concatenate([T - 1.0 - s, -1.0 - s, s + 1.0, full])
    n_b = jnp.concatenate([s, s - T, T - s, full])
    zf = jnp.exp(n_f[:, None] * ldt[0].reshape(1, G * P))
    zb = jnp.exp(n_b[:, None] * ldt[1].reshape(1, G * P))
    pw = lax.optimization_barrier(jnp.stack([jnp.real(zf), jnp.imag(zf), jnp.real(zb), jnp.imag(zb)]))
    pw = jnp.transpose(pw.reshape(4, 3 * T + 1, G, P), (2, 0, 3, 1)).reshape(G, 4 * P, 3 * T + 1)

    def rows(zf, zb):
        return jnp.concatenate([jnp.real(zf), jnp.imag(zf), jnp.real(zb), jnp.imag(zb)], axis=1)

    gap = jnp.zeros((G, 4 * P, S5_F_B - (3 * T + 1)), F32)
    small = jnp.concatenate([pw, gap, rows(bbar[0], bbar[1]), rows(c[0], c[1])], axis=2)
    return jnp.pad(small, ((0, 0), (0, 0), (0, 128 - small.shape[2])))


S5_F_INC, S5_F_KIN, S5_F_OUT, S5_F_A, S5_F_B, S5_F_C = 0, S5_T, 2 * S5_T, 3 * S5_T, 4 * S5_T, 5 * S5_T


def _cmul_rows(x, ar, ai):
    h = x.shape[0] // 2
    xr, xi = x[:h], x[h:]
    return jnp.concatenate([xr * ar - xi * ai, xr * ai + xi * ar], axis=0)


def _s5_lane_scan(inc, a_rows, h0, lc):
    P = S5_P
    cidx = lax.broadcasted_iota(jnp.int32, (2 * P, 128), 1) & (lc - 1)
    fwd, bwd = inc[:2 * P], inc[2 * P:]
    arf, aif, arb, aib = a_rows[:P], a_rows[P:2 * P], a_rows[2 * P:3 * P], a_rows[3 * P:]
    if h0 is not None:
        fwd = fwd + jnp.where(cidx == 0, _cmul_rows(h0[:2 * P], arf, aif), 0.0)
        bwd = bwd + jnp.where(cidx == lc - 1, _cmul_rows(h0[2 * P:], arb, aib), 0.0)
    d = 1
    while d < lc:
        sh = jnp.where(cidx >= d, pltpu.roll(fwd, d, axis=1), 0.0)
        fwd = fwd + _cmul_rows(sh, arf, aif)
        sh = jnp.where(cidx < lc - d, pltpu.roll(bwd, 128 - d, axis=1), 0.0)
        bwd = bwd + _cmul_rows(sh, arb, aib)
        d *= 2
        if d < lc:
            arf, aif = arf * arf - aif * aif, 2.0 * arf * aif
            arb, aib = arb * arb - aib * aib, 2.0 * arb * aib
    first_f = 0.0 if h0 is None else h0[:2 * P]
    first_b = 0.0 if h0 is None else h0[2 * P:]
    start_f = jnp.where(cidx >= 1, pltpu.roll(fwd, 1, axis=1), first_f)
    start_b = jnp.where(cidx < lc - 1, pltpu.roll(bwd, 127, axis=1), first_b)
    return jnp.concatenate([start_f, start_b], axis=0), jnp.concatenate([fwd, bwd], axis=0)


def _dot_split(a, b):
    a_hi = a.astype(BF16)
    b_hi = b.astype(BF16)
    a_lo = (a - a_hi.astype(F32)).astype(BF16)
    b_lo = (b - b_hi.astype(F32)).astype(BF16)
    return _dot(a_hi, b_hi) + (_dot(a_hi, b_lo) + _dot(a_lo, b_hi))


S5_SLAB = 128 // S5_GROUP


def _split3(x):
    hi = x.astype(BF16)
    r = x - hi.astype(F32)
    mid = r.astype(BF16)
    return hi, mid, (r - mid.astype(F32)).astype(BF16)


def _cprod_rows(a, b, conj_sign):
    h = a.shape[0] // 2
    ar, ai, br, bi = a[:h], a[h:], b[:h], b[h:]
    return jnp.concatenate([ar * br - ai * bi, conj_sign * (ar * bi + ai * br)], axis=0)


def _s5_kernel(u_ref, f_ref, h0_ref, y_ref, fin_ref, xt_ref, yt_ref):
    T, K, P, R = S5_T, S5_GROUP, S5_P, S5_ROWS
    F = S5_LANES
    for s in range(T):
        ut = u_ref[pl.ds(s, R, stride=T), :].T
        for gg in range(S5_SLAB):
            xt_ref[gg, s * K:(s + 1) * K, :] = ut[gg * K:(gg + 1) * K, :].astype(BF16)
    shift = K.bit_length() - 1
    tok_out = lax.broadcasted_iota(jnp.int32, (F, F), 0) >> shift
    tok_in = lax.broadcasted_iota(jnp.int32, (F, F), 1) >> shift
    lane = lax.broadcasted_iota(jnp.int32, (128, F), 0)
    feat = lax.broadcasted_iota(jnp.int32, (128, F), 1)
    tok, ch = feat >> shift, feat & (K - 1)

    def spread(field, index):
        return jnp.where(lane == field + index, 1.0, 0.0).astype(BF16)

    e_inc, e_kin, e_out = spread(S5_F_INC, tok), spread(S5_F_KIN, tok), spread(S5_F_OUT, tok)
    e_b, e_c = spread(S5_F_B, ch), spread(S5_F_C, ch)
    lane_a = lax.broadcasted_iota(jnp.int32, (128, 128), 0)
    e_a = jnp.where(lane_a == S5_F_A, 1.0, 0.0).astype(BF16)
    seq = lax.broadcasted_iota(jnp.int32, (128, 128), 1)
    nseq = 128 // S5_CTX_CHUNKS
    sel_last = [jnp.where((lane_a == (seq - o) * S5_CTX_CHUNKS + S5_CTX_CHUNKS - 1) & (seq >= o) & (seq < o + nseq),
                          1.0, 0.0).astype(BF16) for o in (0, nseq)]
    sel_first = [jnp.where((lane_a == (seq - o) * S5_CTX_CHUNKS) & (seq >= o) & (seq < o + nseq),
                           1.0, 0.0).astype(BF16) for o in (0, nseq)]

    def group(gg, carry):
        hi, mid, lo = _split3(f_ref[gg])

        def expand(e, exact=False):
            out = _dot(hi, e) + _dot(mid, e)
            return out + _dot(lo, e) if exact else out

        def cprod(pw, other, conj_sign):
            return jnp.concatenate([_cprod_rows(pw[:2 * P], other[:2 * P], conj_sign),
                                    _cprod_rows(pw[2 * P:], other[2 * P:], conj_sign)], axis=0)

        bb = expand(e_b)
        inc_op = cprod(expand(e_inc), bb, 1.0)
        kin_op = cprod(expand(e_kin), bb, 1.0)
        out_op = cprod(expand(e_out), expand(e_c), -1.0).T
        a_rows = expand(e_a, exact=True)
        resp_f = _dot_split(out_op[:, :2 * P], kin_op[:2 * P])
        resp_b = _dot_split(out_op[:, 2 * P:], kin_op[2 * P:])
        resp = jnp.where(tok_out >= tok_in, resp_f, 0.0) + jnp.where(tok_out <= tok_in, resp_b, 0.0)
        ops = jnp.concatenate([resp, inc_op], axis=0).astype(BF16)
        z = _dot(ops, xt_ref[gg])
        lanes = S5_CTX_ROWS // 2
        st_a, sc_a = _s5_lane_scan(z[F:, :lanes], a_rows, None, S5_CTX_CHUNKS)
        st_b, sc_b = _s5_lane_scan(z[F:, lanes:2 * lanes], a_rows, None, S5_CTX_CHUNKS)
        st_d, _ = _s5_lane_scan(z[F:, 2 * lanes:], a_rows, h0_ref[gg], S5_DEC_CHUNKS)
        start = jnp.concatenate([st_a, st_b, st_d], axis=1).astype(BF16)
        yt_ref[gg] = z[:F] + _dot(out_op.astype(BF16), start)
        fin_f = sum(_dot(t, sel_last[0]) for t in _split3(sc_a[:2 * P])) \
            + sum(_dot(t, sel_last[1]) for t in _split3(sc_b[:2 * P]))
        fin_b = sum(_dot(t, sel_first[0]) for t in _split3(sc_a[2 * P:])) \
            + sum(_dot(t, sel_first[1]) for t in _split3(sc_b[2 * P:]))
        fin_ref[gg] = jnp.concatenate([fin_f, fin_b], axis=0)
        return carry

    lax.fori_loop(0, S5_SLAB, group, 0)
    for i in range(T):
        yt = jnp.concatenate([yt_ref[gg, i * K:(i + 1) * K, :] for gg in range(S5_SLAB)], axis=0)
        y_ref[pl.ds(i, R, stride=T), :] = yt.T


def _s5_scan(proj, factors, h0):
    G, F, P4 = S5_GROUPS, S5_LANES, 4 * S5_P
    assert S5_CTX_ROWS == 256 and S5_DEC_ROWS == 128 and F == 256 and P4 == 256 and S5_T == S5_GROUP
    op_spec = pl.BlockSpec((S5_SLAB, P4, 128), lambda j: (j, 0, 0))
    return pl.pallas_call(
        _s5_kernel,
        grid=(G // S5_SLAB,),
        in_specs=[pl.BlockSpec((N_TOK, 128), lambda j: (0, j)), op_spec, op_spec],
        out_specs=[pl.BlockSpec((N_TOK, 128), lambda j: (0, j)), op_spec],
        out_shape=[
            jax.ShapeDtypeStruct((N_TOK, S5_WIDTH), F32),
            jax.ShapeDtypeStruct((G, P4, 128), F32),
        ],
        scratch_shapes=[pltpu.VMEM((S5_SLAB, F, S5_ROWS), BF16), pltpu.VMEM((S5_SLAB, F, S5_ROWS), F32)],
        compiler_params=_cparams(("parallel",)),
        name="s5_scan",
    )(proj, factors, h0)


S5_POST_TM = 512


def _s5_post_kernel(y_ref, u_ref, d_ref, w_ref, b_ref, o_ref):
    y = y_ref[...] + d_ref[...] * u_ref[...]
    z = 0.5 * y * (1.0 + jnp.tanh(math.sqrt(2.0 / math.pi) * (y + 0.044715 * (y * y * y))))
    gate = jax.nn.sigmoid(_dot(z.astype(BF16), w_ref[...].astype(BF16)) + b_ref[...])
    o_ref[...] = (z * gate).astype(BF16)


def _s5_post(y, proj, d_skip, w_glu, b_glu, e):
    tm = S5_POST_TM
    return pl.pallas_call(
        _s5_post_kernel,
        grid=(N_TOK // tm,),
        in_specs=[
            pl.BlockSpec((tm, S5_WIDTH), lambda i: (i, 0)),
            pl.BlockSpec((tm, S5_WIDTH), lambda i: (i, 0)),
            pl.BlockSpec((1, S5_WIDTH), lambda i: (0, 0)),
            pl.BlockSpec((None, S5_WIDTH, S5_WIDTH), lambda i: (e, 0, 0), pipeline_mode=pl.Buffered(1)),
            pl.BlockSpec((1, S5_WIDTH), lambda i: (0, 0)),
        ],
        out_specs=pl.BlockSpec((tm, S5_WIDTH), lambda i: (i, 0)),
        out_shape=jax.ShapeDtypeStruct((N_TOK, S5_WIDTH), BF16),
        compiler_params=_cparams(("parallel",)),
        name="s5_post",
    )(y, proj, d_skip.reshape(1, S5_WIDTH), w_glu, b_glu.reshape(1, S5_WIDTH))


def _s5_mixer(proj, ops, h0, d_skip, w_glu, b_glu, e):
    y, fin = _s5_scan(proj, ops, h0)
    y_a = _s5_post(y, proj, d_skip, w_glu, b_glu, e)
    fin = fin[:, :, :BATCH].reshape(S5_GROUPS, 2, 2, S5_P, BATCH)
    fin = jnp.transpose(fin, (4, 1, 2, 0, 3))
    return y_a, fin[:, :, 0], fin[:, :, 1]


def _ret_head(lgf, lgb, q_ref, k_ref, v_ref, g_ref, gn_ref, o_ref, s0_f, s0_b, cols, seq_len):
    C = RET_CHUNK
    nc = seq_len // C
    row = lax.broadcasted_iota(jnp.int32, (C, C), 0).astype(F32)
    col = lax.broadcasted_iota(jnp.int32, (C, C), 1).astype(F32)
    rel = row - col
    decay = (jnp.where(rel >= 0, jnp.exp(jnp.maximum(rel, 0.0) * lgf), 0.0)
             + jnp.where(rel <= 0, jnp.exp(jnp.maximum(-rel, 0.0) * lgb), 0.0))
    kdec_f = jnp.exp((C - 1.0 - row) * lgf)
    kdec_b = jnp.exp(row * lgb)
    qdec_f = jnp.exp((row + 1.0) * lgf)
    qdec_b = jnp.exp((C - row) * lgb)
    cd_f = jnp.exp(jnp.full((1, RET_DV), C * lgf, F32))
    cd_b = jnp.exp(jnp.full((1, RET_DV), C * lgb, F32))

    def chunk(ref, c):
        return ref[c * C:(c + 1) * C, cols]

    tdot = functools.partial(lax.dot_general, dimension_numbers=(((0,), (0,)), ((), ())),
                             preferred_element_type=F32)
    outs = []
    kv_f, kv_b = [], []
    for c in range(nc):
        q = chunk(q_ref, c)
        k = chunk(k_ref, c) * (RET_DK ** -0.5)
        v = chunk(v_ref, c).astype(BF16)
        inner = lax.dot_general(q.astype(BF16), k.astype(BF16), (((1,), (1,)), ((), ())),
                                preferred_element_type=F32) * decay
        outs.append(_dot(inner.astype(BF16), v))
        kv_f.append(tdot((k * kdec_f).astype(BF16), v))
        kv_b.append(tdot((k * kdec_b).astype(BF16), v))
    s_f = s0_f
    for c in range(nc):
        outs[c] = outs[c] + _dot((chunk(q_ref, c) * qdec_f).astype(BF16), s_f.astype(BF16))
        s_f = cd_f * s_f + kv_f[c]
    s_b = s0_b
    for c in reversed(range(nc)):
        outs[c] = outs[c] + _dot((chunk(q_ref, c) * qdec_b).astype(BF16), s_b.astype(BF16))
        s_b = cd_b * s_b + kv_b[c]
    for c in range(nc):
        o = outs[c]
        mu = jnp.mean(o, axis=-1, keepdims=True)
        var = jnp.mean(jnp.square(o - mu), axis=-1, keepdims=True)
        o = ((o - mu) * lax.rsqrt(var + EPS)) * gn_ref[:, cols]
        g = chunk(g_ref, c)
        o_ref[c * C:(c + 1) * C, cols] = ((g * jax.nn.sigmoid(g)) * o).astype(BF16)
    return s_f, s_b


def _ret_kernel(lg_ref, q_ref, k_ref, v_ref, g_ref, gn_ref, *rest, seq_len, heads, has_state):
    if has_state:
        s0_ref, o_ref = rest
    else:
        o_ref, sfin_ref = rest
    hg = pl.program_id(1)
    for j in range(heads):
        hd = hg * heads + j
        cols = slice(j * RET_DK, (j + 1) * RET_DK)
        if has_state:
            s0_f, s0_b = s0_ref[0, j], s0_ref[1, j]
        else:
            s0_f = s0_b = jnp.zeros((RET_DK, RET_DV), F32)
        s_f, s_b = _ret_head(lg_ref[0, hd], lg_ref[1, hd], q_ref, k_ref, v_ref, g_ref, gn_ref, o_ref,
                             s0_f, s0_b, cols, seq_len)
        if not has_state:
            sfin_ref[0, j] = s_f
            sfin_ref[1, j] = s_b


def _retention(proj, lg, gn_g, s0, n_seq, seq_len, blk0, heads):
    H = RET_HEADS
    width = heads * RET_DK

    def col_spec(off):
        first = (S5_WIDTH + off * RET_WIDTH) // width
        return pl.BlockSpec((seq_len, width), lambda b, hg: (blk0 + b, first + hg))

    state_spec = pl.BlockSpec((None, 2, heads, RET_DK, RET_DV), lambda b, hg: (b, 0, hg, 0, 0))
    in_specs = [
        pl.BlockSpec(memory_space=pltpu.SMEM),
        col_spec(0), col_spec(1), col_spec(2), col_spec(3),
        pl.BlockSpec((1, width), lambda b, hg: (0, hg)),
    ]
    args = [lg, proj, proj, proj, proj, gn_g.reshape(1, RET_WIDTH)]
    out_specs = [pl.BlockSpec((seq_len, width), lambda b, hg: (b, hg))]
    out_shape = [jax.ShapeDtypeStruct((n_seq * seq_len, RET_WIDTH), BF16)]
    if s0 is not None:
        in_specs.append(state_spec)
        args.append(s0)
    else:
        out_specs.append(state_spec)
        out_shape.append(jax.ShapeDtypeStruct((n_seq, 2, H, RET_DK, RET_DV), F32))
    return pl.pallas_call(
        functools.partial(_ret_kernel, seq_len=seq_len, heads=heads, has_state=s0 is not None),
        grid=(n_seq, H // heads),
        in_specs=in_specs,
        out_specs=out_specs,
        out_shape=out_shape,
        compiler_params=_cparams(("parallel", "parallel")),
        name="retention_dec" if s0 is not None else "retention_ctx",
    )(*args)


def _shift_rows(x, d, ridx):
    n = x.shape[0]
    if d == 0:
        return x
    rolled = pltpu.roll(x, d % n, axis=0)
    valid = (ridx >= d) if d > 0 else (ridx < n + d)
    return jnp.where(valid, rolled, 0.0)


def _pool_kernel(u_ref, w_ref, sc_ref, o_ref, *, seq_len):
    L = seq_len
    ridx = lax.broadcasted_iota(jnp.int32, (L, POOL_GC), 0)
    for gi, w in enumerate(POOL_WINDOWS):
        cols = slice(gi * POOL_GC, (gi + 1) * POOL_GC)
        x = u_ref[:, cols]
        half = w // 2
        back, ahead = x, x
        span = 1
        while span < half:
            back = back + _shift_rows(back, span, ridx)
            ahead = ahead + _shift_rows(ahead, -span, ridx)
            span *= 2
        win = _shift_rows(back, 1, ridx) + ahead
        lo = jnp.maximum(ridx - half, 0)
        hi = jnp.minimum(ridx + half - 1, L - 1)
        mean = win / (hi - lo + 1).astype(F32)
        mixed = _dot((mean - x).astype(BF16), w_ref[gi].astype(BF16))
        o_ref[:, cols] = (mixed * sc_ref[:, cols]).astype(BF16)


def _pool(proj, w_pool, o, pool_scale, n_seq, seq_len, blk0):
    return pl.pallas_call(
        functools.partial(_pool_kernel, seq_len=seq_len),
        grid=(n_seq,),
        in_specs=[
            pl.BlockSpec((seq_len, POOL_WIDTH), lambda b: (blk0 + b, 0)),
            pl.BlockSpec((None, POOL_GROUPS, POOL_GC, POOL_GC), lambda b: (o, 0, 0, 0)),
            pl.BlockSpec((1, POOL_WIDTH), lambda b: (0, 0)),
        ],
        out_specs=pl.BlockSpec((seq_len, POOL_WIDTH), lambda b: (b, 0)),
        out_shape=jax.ShapeDtypeStruct((n_seq * seq_len, POOL_WIDTH), BF16),
        compiler_params=_cparams(("parallel",)),
        name="pool",
    )(proj, w_pool, pool_scale.reshape(1, POOL_WIDTH))


def _softmax_pv(s, v, sink):
    m = jnp.maximum(jnp.max(s, axis=-1, keepdims=True), sink)
    p = jnp.exp(s - m)
    denom = jnp.sum(p, axis=-1, keepdims=True) + jnp.exp(sink - m)
    return _dot(p.astype(BF16), v) / denom


def _qk(q, k):
    return lax.dot_general(q, k, (((1,), (1,)), ((), ())), preferred_element_type=F32) * (ATT_HD ** -0.5)


def _ctx_att_kernel(sink_ref, q_ref, k_ref, v_ref, o_ref, ko_ref, vo_ref):
    kv = pl.program_id(1)
    ko_ref[...] = k_ref[...]
    vo_ref[...] = v_ref[...]
    k = k_ref[...].astype(BF16)
    v = v_ref[...].astype(BF16)
    for g in range(ATT_GROUP):
        cols = slice(g * ATT_HD, (g + 1) * ATT_HD)
        s = _qk(q_ref[:, cols].astype(BF16), k)
        o_ref[:, cols] = _softmax_pv(s, v, sink_ref[kv * ATT_GROUP + g]).astype(BF16)


def _ctx_attention(proj, sink):
    qw = ATT_GROUP * ATT_HD
    qb = POOL_WIDTH // qw
    kb = (POOL_WIDTH + ATT_HEADS * ATT_HD) // ATT_HD
    return pl.pallas_call(
        _ctx_att_kernel,
        grid=(BATCH, ATT_KV),
        in_specs=[
            pl.BlockSpec(memory_space=pltpu.SMEM),
            pl.BlockSpec((SEQ, qw), lambda b, kv: (b, qb + kv)),
            pl.BlockSpec((SEQ, ATT_HD), lambda b, kv: (b, kb + kv)),
            pl.BlockSpec((SEQ, ATT_HD), lambda b, kv: (b, kb + ATT_KV + kv)),
        ],
        out_specs=[pl.BlockSpec((SEQ, qw), lambda b, kv: (b, kv)),
                   pl.BlockSpec((SEQ, ATT_HD), lambda b, kv: (b, kv)),
                   pl.BlockSpec((SEQ, ATT_HD), lambda b, kv: (b, kv))],
        out_shape=[jax.ShapeDtypeStruct((N_CTX_TOK, ATT_HEADS * ATT_HD), BF16),
                   jax.ShapeDtypeStruct((N_CTX_TOK, ATT_KV * ATT_HD), F32),
                   jax.ShapeDtypeStruct((N_CTX_TOK, ATT_KV * ATT_HD), F32)],
        compiler_params=_cparams(("parallel", "parallel")),
        name="attention_ctx",
    )(sink, proj, proj, proj)


def _rope(x, cos, sin):
    lane = lax.broadcasted_iota(jnp.int32, x.shape, 1)
    partner = jnp.where((lane & 63) < 32, pltpu.roll(x, 96, axis=1), pltpu.roll(x, 32, axis=1))
    return x * cos + partner * sin


def _dec_att_kernel(sink_ref, q_ref, k_ref, v_ref, kc_ref, vc_ref, cos_ref, sin_ref, o_ref, kr_ref):
    kv = pl.program_id(1)
    L, B = DEC_SEQ, ATT_BLOCK
    nb = L // B
    kr_ref[...] = _rope(k_ref[...], cos_ref[...], sin_ref[...]).astype(BF16)
    kc = kc_ref[...].astype(BF16)
    vc = vc_ref[...].astype(BF16)
    for i in range(nb):
        lo, hi = max(i - 1, 0) * B, min(i + 2, nb) * B
        rows = slice(i * B, (i + 1) * B)
        kwin = kr_ref[lo:hi, :]
        vals = jnp.concatenate([v_ref[lo:hi, :].astype(BF16), vc], axis=0)
        qpos = i * B + lax.broadcasted_iota(jnp.int32, (B, hi - lo), 0)
        kpos = lo + lax.broadcasted_iota(jnp.int32, (B, hi - lo), 1)
        band = jnp.abs(kpos - qpos) <= ATT_WIN
        for g in range(ATT_GROUP):
            cols = slice(g * ATT_HD, (g + 1) * ATT_HD)
            q = _rope(q_ref[rows, cols], cos_ref[rows, :], sin_ref[rows, :]).astype(BF16)
            s = jnp.concatenate([jnp.where(band, _qk(q, kwin), NEG), _qk(q, kc)], axis=1)
            o_ref[rows, cols] = _softmax_pv(s, vals, sink_ref[kv * ATT_GROUP + g]).astype(BF16)


def _rope_tables():
    half = ATT_HD // 2
    nfreq = half // 2
    freqs = ROPE_BASE ** (-jnp.arange(nfreq, dtype=F32) / nfreq)
    t = jnp.arange(DEC_SEQ)
    pos = jnp.stack([t // GRID_W, t % GRID_W], axis=1).astype(F32)
    ang = pos[:, :, None] * freqs[None, None, :]
    cos = jnp.repeat(jnp.cos(ang), 2, axis=1).reshape(DEC_SEQ, ATT_HD)
    sin = jnp.sin(ang)
    sin = jnp.stack([-sin, sin], axis=2).reshape(DEC_SEQ, ATT_HD)
    return cos, sin


def _dec_attention(proj, cache_k, cache_v, sink, cos, sin, o):
    qw = ATT_GROUP * ATT_HD
    qb = POOL_WIDTH // qw
    kb = (POOL_WIDTH + ATT_HEADS * ATT_HD) // ATT_HD
    blk0 = N_CTX_TOK // DEC_SEQ
    ck = cache_k.reshape(DEC_BATCH, N_ODD, PAST_LEN, ATT_KV * ATT_HD)
    cv = cache_v.reshape(DEC_BATCH, N_ODD, PAST_LEN, ATT_KV * ATT_HD)
    cache_spec = pl.BlockSpec((None, None, PAST_LEN, ATT_HD), lambda b, kv: (b, o, 0, kv))
    table_spec = pl.BlockSpec((DEC_SEQ, ATT_HD), lambda b, kv: (0, 0))
    return pl.pallas_call(
        _dec_att_kernel,
        grid=(DEC_BATCH, ATT_KV),
        in_specs=[
            pl.BlockSpec(memory_space=pltpu.SMEM),
            pl.BlockSpec((DEC_SEQ, qw), lambda b, kv: (blk0 + b, qb + kv)),
            pl.BlockSpec((DEC_SEQ, ATT_HD), lambda b, kv: (blk0 + b, kb + kv)),
            pl.BlockSpec((DEC_SEQ, ATT_HD), lambda b, kv: (blk0 + b, kb + ATT_KV + kv)),
            cache_spec, cache_spec, table_spec, table_spec,
        ],
        out_specs=pl.BlockSpec((DEC_SEQ, qw), lambda b, kv: (b, kv)),
        out_shape=jax.ShapeDtypeStruct((N_DEC_TOK, ATT_HEADS * ATT_HD), BF16),
        scratch_shapes=[pltpu.VMEM((DEC_SEQ, ATT_HD), BF16)],
        compiler_params=_cparams(("parallel", "parallel")),
        name="attention_dec",
    )(sink, proj, proj, proj, ck, cv, cos, sin)


def kernel(x_prompt, x_sample, c, state_s5_re, state_s5_im, state_ret, cache_k, cache_v, c_ctx, w_mod, b_mod, norm_g, ffn1_gate, ffn1_up, ffn1_down, ffn2_gate, ffn2_up, ffn2_down, even_w_in, even_w_out, s5_lam_re, s5_lam_im, s5_log_dt, s5_b_re, s5_b_im, s5_c_re, s5_c_im, s5_d, s5_glu_w, s5_glu_b, ret_decay_logit, ret_gn_g, odd_w_in, odd_w_out, pool_w, pool_scale, att_sink):
    x = jnp.concatenate([x_prompt.reshape(N_CTX_TOK, D_MODEL), x_sample.reshape(N_DEC_TOK, D_MODEL)], axis=0)
    c8 = jnp.concatenate([c_ctx[None, :], c, jnp.zeros((MOD_ROWS - 1 - DEC_BATCH, D_MODEL), F32)], axis=0)
    mod5 = _modulation(c8, w_mod, b_mod).reshape(DEPTH, MOD_ROWS, N_MOD, 1, D_MODEL)
    norm4 = norm_g.reshape(DEPTH, 6, 1, D_MODEL)
    cos, sin = _rope_tables()
    dec_blk = N_CTX_TOK // DEC_SEQ
    out_re, out_im, out_ret, out_k, out_v = [], [], [], [], []
    for l in range(DEPTH):
        x = _ffn(x, mod5, norm4, ffn1_gate, ffn1_up, ffn1_down, l, 0, 0, 1)
        if l % 2 == 0:
            e = l // 2
            proj = _proj_in(x, mod5, norm4, even_w_in, e, l)
            ops = _s5_operators(s5_lam_re[e], s5_lam_im[e], s5_log_dt[e], s5_b_re[e], s5_b_im[e],
                                s5_c_re[e], s5_c_im[e])
            h0 = jnp.stack([state_s5_re[:, e], state_s5_im[:, e]], axis=2)
            h0 = jnp.transpose(h0, (3, 1, 2, 4, 0)).reshape(S5_GROUPS, 4 * S5_P, DEC_BATCH)
            h0 = jnp.repeat(h0, S5_DEC_CHUNKS, axis=2)
            y_a, s_re, s_im = _s5_mixer(proj, ops, h0, s5_d[e], s5_glu_w, s5_glu_b[e], e)
            lg = jax.nn.log_sigmoid(ret_decay_logit[e].astype(F32))
            yb_ctx, s_ret = _retention(proj, lg, ret_gn_g[e], None, BATCH, SEQ, 0, 4)
            (yb_dec,) = _retention(proj, lg, ret_gn_g[e], state_ret[:, e], DEC_BATCH, DEC_SEQ, dec_blk, 1)
            y_b = (yb_ctx, yb_dec)
            out_re.append(s_re)
            out_im.append(s_im)
            out_ret.append(s_ret)
            w_out, w_idx = even_w_out, e
        else:
            o = l // 2
            proj = _proj_in(x, mod5, norm4, odd_w_in, o, l)
            y_a = (_pool(proj, pool_w, o, pool_scale[o], BATCH, SEQ, 0),
                   _pool(proj, pool_w, o, pool_scale[o], DEC_BATCH, DEC_SEQ, dec_blk))
            sink = att_sink[o].astype(F32)
            yd_ctx, k_ctx, v_ctx = _ctx_attention(proj, sink)
            y_b = (yd_ctx, _dec_attention(proj, cache_k, cache_v, sink, cos, sin, o))
            out_k.append(k_ctx.reshape(BATCH, SEQ, ATT_KV, ATT_HD))
            out_v.append(v_ctx.reshape(BATCH, SEQ, ATT_KV, ATT_HD))
            w_out, w_idx = odd_w_out, o
        x = _proj_out(x, y_a, y_b, mod5, norm4, w_out, w_idx, l)
        x = _ffn(x, mod5, norm4, ffn2_gate, ffn2_up, ffn2_down, l, 2, 4, 5)
    y_prompt = x[:N_CTX_TOK].reshape(BATCH, SEQ, D_MODEL)
    y_sample = x[N_CTX_TOK:].reshape(DEC_BATCH, DEC_SEQ, D_MODEL)
    return (y_prompt, y_sample, jnp.stack(out_re, axis=1), jnp.stack(out_im, axis=1),
            jnp.stack(out_ret, axis=1), jnp.stack(out_k, axis=1), jnp.stack(out_v, axis=1))
```

```python
import functools
import math

import jax
import jax.numpy as jnp
from jax import lax
from jax.experimental import pallas as pl
from jax.experimental.pallas import tpu as pltpu

F32 = jnp.float32
BF16 = jnp.bfloat16

D_MODEL = 2048
BATCH = 16
SEQ = 256
DEPTH = 4
DEC_BATCH = 2
DEC_SEQ = 1024
PAST_LEN = 256
GRID_W = 64
N_EVEN = (DEPTH + 1) // 2
N_ODD = DEPTH // 2
N_MOD = 9
FFN_HIDDEN = 5632
EPS = 1e-6
NEG = -1e30

S5_WIDTH = D_MODEL // 2
S5_GROUP = 16
S5_GROUPS = S5_WIDTH // S5_GROUP
S5_P = 64
RET_HEADS = 8
RET_DK = (D_MODEL // 2) // RET_HEADS
RET_DV = RET_DK
RET_WIDTH = RET_HEADS * RET_DV
RET_CHUNK = 128
EVEN_IN = S5_WIDTH + 4 * RET_WIDTH

POOL_WIDTH = D_MODEL // 2
POOL_WINDOWS = (2, 4, 8, 16)
POOL_GROUPS = 4
POOL_GC = POOL_WIDTH // POOL_GROUPS
ATT_HEADS = 8
ATT_KV = 2
ATT_HD = (D_MODEL // 2) // ATT_HEADS
ATT_GROUP = ATT_HEADS // ATT_KV
ATT_WIN = 128
ATT_BLOCK = 128
ODD_IN = POOL_WIDTH + (ATT_HEADS + 2 * ATT_KV) * ATT_HD
ROPE_BASE = 10000.0

N_CTX_TOK = BATCH * SEQ
N_DEC_TOK = DEC_BATCH * DEC_SEQ
N_TOK = N_CTX_TOK + N_DEC_TOK
MOD_ROWS = 8

S5_T = 16
S5_LANES = S5_T * S5_GROUP
S5_CTX_CHUNKS = SEQ // S5_T
S5_DEC_CHUNKS = DEC_SEQ // S5_T
S5_CTX_ROWS = BATCH * S5_CTX_CHUNKS
S5_DEC_ROWS = DEC_BATCH * S5_DEC_CHUNKS
S5_ROWS = S5_CTX_ROWS + S5_DEC_ROWS
S5_NPOW = 6

VMEM_LIMIT = 60 * 1024 * 1024


def _cparams(sem):
    return pltpu.CompilerParams(dimension_semantics=sem, vmem_limit_bytes=VMEM_LIMIT)


def _dot(a, b):
    return jnp.dot(a, b, preferred_element_type=F32)


def _rms(x, g):
    return (x * lax.rsqrt(jnp.mean(x * x, axis=-1, keepdims=True) + EPS)) * g


def _mod_row(i, tm):
    start = i * tm
    return (start >= N_CTX_TOK).astype(jnp.int32) + (start >= N_CTX_TOK + DEC_SEQ).astype(jnp.int32)


def _mod_kernel(c_ref, w_ref, b_ref, o_ref):
    c = c_ref[...]
    s = (c * jax.nn.sigmoid(c)).astype(BF16)
    o_ref[...] = _dot(s, w_ref[...].astype(BF16)) + b_ref[...]


def _modulation(c8, w_mod, b_mod):
    tn = 1024
    n = N_MOD * D_MODEL
    return pl.pallas_call(
        _mod_kernel,
        grid=(DEPTH, n // tn),
        in_specs=[
            pl.BlockSpec((MOD_ROWS, D_MODEL), lambda l, j: (0, 0)),
            pl.BlockSpec((None, D_MODEL, tn), lambda l, j: (l, 0, j)),
            pl.BlockSpec((None, 1, tn), lambda l, j: (l, 0, j)),
        ],
        out_specs=pl.BlockSpec((None, MOD_ROWS, tn), lambda l, j: (l, 0, j)),
        out_shape=jax.ShapeDtypeStruct((DEPTH, MOD_ROWS, n), F32),
        compiler_params=_cparams(("arbitrary", "arbitrary")),
        name="modulation",
    )(c8, w_mod, b_mod.reshape(DEPTH, 1, n))


def _mod_spec(l, j, tm):
    return pl.BlockSpec((None, None, None, 1, D_MODEL),
                        lambda i, *_: (l, _mod_row(i, tm), j, 0, 0))


def _gain_spec(l, j):
    return pl.BlockSpec((None, None, 1, D_MODEL), lambda i, *_: (l, j, 0, 0))


FFN_TM = 1024
FFN_TF = 256
SLAB_ROWS = 256


def _adaln_in(x, g_ref, sc_ref, sh_ref):
    return (_rms(x, g_ref[...]) * (1.0 + sc_ref[...]) + sh_ref[...]).astype(BF16)


def _ffn_kernel(x_ref, sh_ref, sc_ref, gate_ref, gin_ref, gout_ref, wg_ref, wu_ref, wd_ref,
                o_ref, h_ref, *, nf, tm):
    f = pl.program_id(1)
    slabs = [slice(r0, r0 + SLAB_ROWS) for r0 in range(0, tm, SLAB_ROWS)]

    def weights():
        return wg_ref[...].astype(BF16), wu_ref[...].astype(BF16), wd_ref[...].astype(BF16)

    def swiglu(h, w):
        g = _dot(h, w[0])
        u = _dot(h, w[1])
        a = ((g * jax.nn.sigmoid(g)) * u).astype(BF16)
        return _dot(a, w[2])

    @pl.when(f == 0)
    def _():
        w = weights()
        for rows in slabs:
            h = _adaln_in(x_ref[rows, :], gin_ref, sc_ref, sh_ref)
            h_ref[rows, :] = h
            o_ref[rows, :] = swiglu(h, w)

    @pl.when(jnp.logical_and(f > 0, f < nf - 1))
    def _():
        o_ref[...] += swiglu(h_ref[...], weights())

    @pl.when(f == nf - 1)
    def _():
        w = weights()
        for rows in slabs:
            y = o_ref[rows, :] + swiglu(h_ref[rows, :], w)
            o_ref[rows, :] = x_ref[rows, :] + (0.5 * gate_ref[...]) * _rms(y, gout_ref[...])


def _ffn(x, mod5, norm4, w_gate, w_up, w_down, l, j, g_in, g_out):
    tm, tf = FFN_TM, FFN_TF
    nf = FFN_HIDDEN // tf
    return pl.pallas_call(
        functools.partial(_ffn_kernel, nf=nf, tm=tm),
        grid=(N_TOK // tm, nf),
        in_specs=[
            pl.BlockSpec((tm, D_MODEL), lambda i, f: (i, 0)),
            _mod_spec(l, 3 * j, tm), _mod_spec(l, 3 * j + 1, tm), _mod_spec(l, 3 * j + 2, tm),
            _gain_spec(l, g_in), _gain_spec(l, g_out),
            pl.BlockSpec((None, D_MODEL, tf), lambda i, f: (l, 0, f)),
            pl.BlockSpec((None, D_MODEL, tf), lambda i, f: (l, 0, f)),
            pl.BlockSpec((None, tf, D_MODEL), lambda i, f: (l, f, 0)),
        ],
        out_specs=pl.BlockSpec((tm, D_MODEL), lambda i, f: (i, 0)),
        out_shape=jax.ShapeDtypeStruct((N_TOK, D_MODEL), F32),
        scratch_shapes=[pltpu.VMEM((tm, D_MODEL), BF16)],
        compiler_params=_cparams(("parallel", "arbitrary")),
        name="ffn",
    )(x, mod5, mod5, mod5, norm4, norm4, w_gate, w_up, w_down)


PROJ_TM = 1024
PROJ_TN = 1280


def _proj_in_kernel(x_ref, sh_ref, sc_ref, gin_ref, w_ref, o_ref, h_ref, *, tm):
    k = pl.program_id(1)

    @pl.when(k == 0)
    def _():
        w = w_ref[...].astype(BF16)
        for r0 in range(0, tm, SLAB_ROWS):
            rows = slice(r0, r0 + SLAB_ROWS)
            h = _adaln_in(x_ref[rows, :], gin_ref, sc_ref, sh_ref)
            h_ref[rows, :] = h
            o_ref[rows, :] = _dot(h, w)

    @pl.when(k > 0)
    def _():
        o_ref[...] = _dot(h_ref[...], w_ref[...].astype(BF16))


def _proj_in(x, mod5, norm4, w_in, e, l):
    tm, tn = PROJ_TM, PROJ_TN
    n = w_in.shape[-1]
    return pl.pallas_call(
        functools.partial(_proj_in_kernel, tm=tm),
        grid=(N_TOK // tm, n // tn),
        in_specs=[
            pl.BlockSpec((tm, D_MODEL), lambda i, k: (i, 0)),
            _mod_spec(l, 3, tm), _mod_spec(l, 4, tm),
            _gain_spec(l, 2),
            pl.BlockSpec((None, D_MODEL, tn), lambda i, k: (e, 0, k)),
        ],
        out_specs=pl.BlockSpec((tm, tn), lambda i, k: (i, k)),
        out_shape=jax.ShapeDtypeStruct((N_TOK, n), F32),
        scratch_shapes=[pltpu.VMEM((tm, D_MODEL), BF16)],
        compiler_params=_cparams(("parallel", "arbitrary")),
        name="proj_in",
    )(x, mod5, mod5, norm4, w_in)


OUT_TM = 512


OUT_ROWS = 256


def _proj_out_kernel(x_ref, yac_ref, yad_ref, ybc_ref, ybd_ref, gate_ref, gout_ref, w_ref, o_ref, *, tm):
    def run(ya_ref, yb_ref):
        w = w_ref[...].astype(BF16)
        for r0 in range(0, tm, OUT_ROWS):
            rows = slice(r0, r0 + OUT_ROWS)
            y = _dot(jnp.concatenate([ya_ref[rows, :], yb_ref[rows, :]], axis=1), w)
            o_ref[rows, :] = x_ref[rows, :] + gate_ref[...] * _rms(y, gout_ref[...])

    is_ctx = pl.program_id(0) < N_CTX_TOK // tm
    pl.when(is_ctx)(lambda: run(yac_ref, ybc_ref))
    pl.when(jnp.logical_not(is_ctx))(lambda: run(yad_ref, ybd_ref))


def _proj_out(x, ya, yb, mod5, norm4, w_out, e, l):
    tm = OUT_TM
    half = D_MODEL // 2
    nct = N_CTX_TOK // tm

    def split_specs(y):
        ctx_spec = pl.BlockSpec((tm, half), lambda i: (jnp.minimum(i, nct - 1), 0))
        if isinstance(y, tuple):
            return y, (ctx_spec, pl.BlockSpec((tm, half), lambda i: (jnp.maximum(i - nct, 0), 0)))
        return (y, y), (ctx_spec, pl.BlockSpec((tm, half), lambda i: (jnp.maximum(i, nct), 0)))

    (ya_c, ya_d), (sa_c, sa_d) = split_specs(ya)
    (yb_c, yb_d), (sb_c, sb_d) = split_specs(yb)
    return pl.pallas_call(
        functools.partial(_proj_out_kernel, tm=tm),
        grid=(N_TOK // tm,),
        in_specs=[
            pl.BlockSpec((tm, D_MODEL), lambda i: (i, 0)),
            sa_c, sa_d, sb_c, sb_d,
            _mod_spec(l, 5, tm),
            _gain_spec(l, 3),
            pl.BlockSpec((None, D_MODEL, D_MODEL), lambda i: (e, 0, 0), pipeline_mode=pl.Buffered(1)),
        ],
        out_specs=pl.BlockSpec((tm, D_MODEL), lambda i: (i, 0)),
        out_shape=jax.ShapeDtypeStruct((N_TOK, D_MODEL), F32),
        compiler_params=_cparams(("parallel",)),
        name="proj_out",
    )(x, ya_c, ya_d, yb_c, yb_d, mod5, norm4, w_out)


def _s5_operators(lam_re, lam_im, log_dt, b_re, b_im, c_re, c_im):
    T, P, G = S5_T, S5_P, S5_GROUPS
    lam = lax.complex(jnp.minimum(lam_re.astype(F32), -1e-4), lam_im.astype(F32))
    ldt = lam * jnp.exp(log_dt.astype(F32))[..., None]
    a = jnp.exp(ldt)
    b_t = jnp.swapaxes(lax.complex(b_re.astype(F32), b_im.astype(F32)), 2, 3)
    bbar = ((a - 1.0) / lam)[:, :, None, :] * b_t
    c = lax.complex(c_re.astype(F32), c_im.astype(F32))
    s = jnp.arange(T, dtype=F32)
    full = jnp.full((1,), T, F32)
    n_pw = 3 * T + 1
    n_f = jnp.concatenate([T - 1.0 - s, -1.0 - s, s + 1.0, full])
    n_b = jnp.concatenate([s, s - T, T - s, full])
    zf = jnp.exp(n_f[:, None] * ldt[0].reshape(1, G * P))
    zb = jnp.exp(n_b[:, None] * ldt[1].reshape(1, G * P))
    pw = lax.optimization_barrier(jnp.stack([jnp.real(zf), jnp.imag(zf), jnp.real(zb), jnp.imag(zb)], axis=1))
    pw = jnp.transpose(pw.reshape(n_pw, 4, G, P), (2, 0, 1, 3)).reshape(G, n_pw, 4 * P)

    def lanes(zf, zb):
        return jnp.concatenate([jnp.real(zf), jnp.imag(zf), jnp.real(zb), jnp.imag(zb)], axis=2)

    gap = jnp.zeros((G, S5_F_B - n_pw, 4 * P), F32)
    fields = jnp.concatenate([pw, gap, lanes(bbar[0], bbar[1]), lanes(c[0], c[1])], axis=1)
    return jnp.pad(fields, ((0, 0), (0, 128 - fields.shape[1]), (0, 0)))


S5_F_INC, S5_F_KIN, S5_F_OUT, S5_F_A, S5_F_B, S5_F_C = 0, S5_T, 2 * S5_T, 3 * S5_T, 4 * S5_T, 5 * S5_T


def _cmul_rows(x, ar, ai):
    h = x.shape[0] // 2
    xr, xi = x[:h], x[h:]
    return jnp.concatenate([xr * ar - xi * ai, xr * ai + xi * ar], axis=0)


def _s5_lane_scan(inc, a_rows, h0, lc):
    P = S5_P
    cidx = lax.broadcasted_iota(jnp.int32, (2 * P, 128), 1) & (lc - 1)
    fwd, bwd = inc[:2 * P], inc[2 * P:]
    arf, aif, arb, aib = a_rows[:P], a_rows[P:2 * P], a_rows[2 * P:3 * P], a_rows[3 * P:]
    if h0 is not None:
        fwd = fwd + jnp.where(cidx == 0, _cmul_rows(h0[:2 * P], arf, aif), 0.0)
        bwd = bwd + jnp.where(cidx == lc - 1, _cmul_rows(h0[2 * P:], arb, aib), 0.0)
    d = 1
    while d < lc:
        sh = jnp.where(cidx >= d, pltpu.roll(fwd, d, axis=1), 0.0)
        fwd = fwd + _cmul_rows(sh, arf, aif)
        sh = jnp.where(cidx < lc - d, pltpu.roll(bwd, 128 - d, axis=1), 0.0)
        bwd = bwd + _cmul_rows(sh, arb, aib)
        d *= 2
        if d < lc:
            arf, aif = arf * arf - aif * aif, 2.0 * arf * aif
            arb, aib = arb * arb - aib * aib, 2.0 * arb * aib
    first_f = 0.0 if h0 is None else h0[:2 * P]
    first_b = 0.0 if h0 is None else h0[2 * P:]
    start_f = jnp.where(cidx >= 1, pltpu.roll(fwd, 1, axis=1), first_f)
    start_b = jnp.where(cidx < lc - 1, pltpu.roll(bwd, 127, axis=1), first_b)
    return jnp.concatenate([start_f, start_b], axis=0), jnp.concatenate([fwd, bwd], axis=0)


def _dot_split(a, b):
    a_hi = a.astype(BF16)
    b_hi = b.astype(BF16)
    a_lo = (a - a_hi.astype(F32)).astype(BF16)
    b_lo = (b - b_hi.astype(F32)).astype(BF16)
    return _dot(a_hi, b_hi) + (_dot(a_hi, b_lo) + _dot(a_lo, b_hi))


S5_SLAB = 128 // S5_GROUP


def _split3(x):
    hi = x.astype(BF16)
    r = x - hi.astype(F32)
    mid = r.astype(BF16)
    return hi, mid, (r - mid.astype(F32)).astype(BF16)


def _cprod_rows(a, b, conj_sign):
    h = a.shape[0] // 2
    ar, ai, br, bi = a[:h], a[h:], b[:h], b[h:]
    return jnp.concatenate([ar * br - ai * bi, conj_sign * (ar * bi + ai * br)], axis=0)


def _s5_kernel(u_ref, f_ref, h0_ref, y_ref, fin_ref, xt_ref, yt_ref):
    T, K, P, R = S5_T, S5_GROUP, S5_P, S5_ROWS
    F = S5_LANES
    for s in range(T):
        ut = u_ref[pl.ds(s, R, stride=T), :].T
        for gg in range(S5_SLAB):
            xt_ref[gg, s * K:(s + 1) * K, :] = ut[gg * K:(gg + 1) * K, :].astype(BF16)
    shift = K.bit_length() - 1
    tok_out = lax.broadcasted_iota(jnp.int32, (F, F), 0) >> shift
    tok_in = lax.broadcasted_iota(jnp.int32, (F, F), 1) >> shift
    lane = lax.broadcasted_iota(jnp.int32, (128, F), 0)
    feat = lax.broadcasted_iota(jnp.int32, (128, F), 1)
    tok, ch = feat >> shift, feat & (K - 1)

    def spread(field, index):
        return jnp.where(lane == field + index, 1.0, 0.0).astype(BF16)

    e_inc, e_kin, e_out = spread(S5_F_INC, tok), spread(S5_F_KIN, tok), spread(S5_F_OUT, tok)
    e_b, e_c = spread(S5_F_B, ch), spread(S5_F_C, ch)
    lane_a = lax.broadcasted_iota(jnp.int32, (128, 128), 0)
    e_a = jnp.where(lane_a == S5_F_A, 1.0, 0.0).astype(BF16)
    seq = lax.broadcasted_iota(jnp.int32, (128, 128), 1)
    nseq = 128 // S5_CTX_CHUNKS
    sel_last = [jnp.where((lane_a == (seq - o) * S5_CTX_CHUNKS + S5_CTX_CHUNKS - 1) & (seq >= o) & (seq < o + nseq),
                          1.0, 0.0).astype(BF16) for o in (0, nseq)]
    sel_first = [jnp.where((lane_a == (seq - o) * S5_CTX_CHUNKS) & (seq >= o) & (seq < o + nseq),
                           1.0, 0.0).astype(BF16) for o in (0, nseq)]

    def group(gg, carry):
        hi, mid, lo = _split3(f_ref[gg].T)

        def expand(e, exact=False):
            out = _dot(hi, e) + _dot(mid, e)
            return out + _dot(lo, e) if exact else out

        def cprod(pw, other, conj_sign):
            return jnp.concatenate([_cprod_rows(pw[:2 * P], other[:2 * P], conj_sign),
                                    _cprod_rows(pw[2 * P:], other[2 * P:], conj_sign)], axis=0)

        bb = expand(e_b)
        inc_op = cprod(expand(e_inc), bb, 1.0)
        kin_op = cprod(expand(e_kin), bb, 1.0)
        out_op = cprod(expand(e_out), expand(e_c), -1.0).T
        a_rows = expand(e_a, exact=True)
        resp_f = _dot_split(out_op[:, :2 * P], kin_op[:2 * P])
        resp_b = _dot_split(out_op[:, 2 * P:], kin_op[2 * P:])
        resp = jnp.where(tok_out >= tok_in, resp_f, 0.0) + jnp.where(tok_out <= tok_in, resp_b, 0.0)
        ops = jnp.concatenate([resp, inc_op], axis=0).astype(BF16)
        z = _dot(ops, xt_ref[gg])
        lanes = S5_CTX_ROWS // 2
        st_a, sc_a = _s5_lane_scan(z[F:, :lanes], a_rows, None, S5_CTX_CHUNKS)
        st_b, sc_b = _s5_lane_scan(z[F:, lanes:2 * lanes], a_rows, None, S5_CTX_CHUNKS)
        st_d, _ = _s5_lane_scan(z[F:, 2 * lanes:], a_rows, h0_ref[gg], S5_DEC_CHUNKS)
        start = jnp.concatenate([st_a, st_b, st_d], axis=1).astype(BF16)
        yt_ref[gg] = z[:F] + _dot(out_op.astype(BF16), start)
        fin_f = sum(_dot(t, sel_last[0]) for t in _split3(sc_a[:2 * P])) \
            + sum(_dot(t, sel_last[1]) for t in _split3(sc_b[:2 * P]))
        fin_b = sum(_dot(t, sel_first[0]) for t in _split3(sc_a[2 * P:])) \
            + sum(_dot(t, sel_first[1]) for t in _split3(sc_b[2 * P:]))
        fin_ref[gg] = jnp.concatenate([fin_f, fin_b], axis=0)
        return carry

    lax.fori_loop(0, S5_SLAB, group, 0)
    for i in range(T):
        yt = jnp.concatenate([yt_ref[gg, i * K:(i + 1) * K, :] for gg in range(S5_SLAB)], axis=0)
        y_ref[pl.ds(i, R, stride=T), :] = yt.T


def _s5_scan(proj, factors, h0):
    G, F, P4 = S5_GROUPS, S5_LANES, 4 * S5_P
    assert S5_CTX_ROWS == 256 and S5_DEC_ROWS == 128 and F == 256 and P4 == 256 and S5_T == S5_GROUP
    op_spec = pl.BlockSpec((S5_SLAB, P4, 128), lambda j: (j, 0, 0))
    factor_spec = pl.BlockSpec((S5_SLAB, 128, P4), lambda j: (j, 0, 0))
    return pl.pallas_call(
        _s5_kernel,
        grid=(G // S5_SLAB,),
        in_specs=[pl.BlockSpec((N_TOK, 128), lambda j: (0, j)), factor_spec, op_spec],
        out_specs=[pl.BlockSpec((N_TOK, 128), lambda j: (0, j)), op_spec],
        out_shape=[
            jax.ShapeDtypeStruct((N_TOK, S5_WIDTH), F32),
            jax.ShapeDtypeStruct((G, P4, 128), F32),
        ],
        scratch_shapes=[pltpu.VMEM((S5_SLAB, F, S5_ROWS), BF16), pltpu.VMEM((S5_SLAB, F, S5_ROWS), F32)],
        compiler_params=_cparams(("parallel",)),
        name="s5_scan",
    )(proj, factors, h0)


S5_POST_TM = 512


def _s5_post_kernel(y_ref, u_ref, d_ref, w_ref, b_ref, o_ref):
    y = y_ref[...] + d_ref[...] * u_ref[...]
    z = 0.5 * y * (1.0 + jnp.tanh(math.sqrt(2.0 / math.pi) * (y + 0.044715 * (y * y * y))))
    gate = jax.nn.sigmoid(_dot(z.astype(BF16), w_ref[...].astype(BF16)) + b_ref[...])
    o_ref[...] = (z * gate).astype(BF16)


def _s5_post(y, proj, d_skip, w_glu, b_glu, e):
    tm = S5_POST_TM
    return pl.pallas_call(
        _s5_post_kernel,
        grid=(N_TOK // tm,),
        in_specs=[
            pl.BlockSpec((tm, S5_WIDTH), lambda i: (i, 0)),
            pl.BlockSpec((tm, S5_WIDTH), lambda i: (i, 0)),
            pl.BlockSpec((1, S5_WIDTH), lambda i: (0, 0)),
            pl.BlockSpec((None, S5_WIDTH, S5_WIDTH), lambda i: (e, 0, 0), pipeline_mode=pl.Buffered(1)),
            pl.BlockSpec((1, S5_WIDTH), lambda i: (0, 0)),
        ],
        out_specs=pl.BlockSpec((tm, S5_WIDTH), lambda i: (i, 0)),
        out_shape=jax.ShapeDtypeStruct((N_TOK, S5_WIDTH), BF16),
        compiler_params=_cparams(("parallel",)),
        name="s5_post",
    )(y, proj, d_skip.reshape(1, S5_WIDTH), w_glu, b_glu.reshape(1, S5_WIDTH))


def _s5_mixer(proj, ops, h0, d_skip, w_glu, b_glu, e):
    y, fin = _s5_scan(proj, ops, h0)
    y_a = _s5_post(y, proj, d_skip, w_glu, b_glu, e)
    fin = fin[:, :, :BATCH].reshape(S5_GROUPS, 2, 2, S5_P, BATCH)
    fin = jnp.transpose(fin, (4, 1, 2, 0, 3))
    return y_a, fin[:, :, 0], fin[:, :, 1]


def _ret_head(lgf, lgb, q_ref, k_ref, v_ref, g_ref, gn_ref, o_ref, s0_f, s0_b, cols, seq_len):
    C = RET_CHUNK
    nc = seq_len // C
    row = lax.broadcasted_iota(jnp.int32, (C, C), 0).astype(F32)
    col = lax.broadcasted_iota(jnp.int32, (C, C), 1).astype(F32)
    rel = row - col
    decay = (jnp.where(rel >= 0, jnp.exp(jnp.maximum(rel, 0.0) * lgf), 0.0)
             + jnp.where(rel <= 0, jnp.exp(jnp.maximum(-rel, 0.0) * lgb), 0.0))
    kdec_f = jnp.exp((C - 1.0 - row) * lgf)
    kdec_b = jnp.exp(row * lgb)
    qdec_f = jnp.exp((row + 1.0) * lgf)
    qdec_b = jnp.exp((C - row) * lgb)
    cd_f = jnp.exp(jnp.full((1, RET_DV), C * lgf, F32))
    cd_b = jnp.exp(jnp.full((1, RET_DV), C * lgb, F32))

    def chunk(ref, c):
        return ref[c * C:(c + 1) * C, cols]

    tdot = functools.partial(lax.dot_general, dimension_numbers=(((0,), (0,)), ((), ())),
                             preferred_element_type=F32)
    outs = []
    kv_f, kv_b = [], []
    for c in range(nc):
        q = chunk(q_ref, c)
        k = chunk(k_ref, c) * (RET_DK ** -0.5)
        v = chunk(v_ref, c).astype(BF16)
        inner = lax.dot_general(q.astype(BF16), k.astype(BF16), (((1,), (1,)), ((), ())),
                                preferred_element_type=F32) * decay
        outs.append(_dot(inner.astype(BF16), v))
        kv_f.append(tdot((k * kdec_f).astype(BF16), v))
        kv_b.append(tdot((k * kdec_b).astype(BF16), v))
    s_f = s0_f
    for c in range(nc):
        outs[c] = outs[c] + _dot((chunk(q_ref, c) * qdec_f).astype(BF16), s_f.astype(BF16))
        s_f = cd_f * s_f + kv_f[c]
    s_b = s0_b
    for c in reversed(range(nc)):
        outs[c] = outs[c] + _dot((chunk(q_ref, c) * qdec_b).astype(BF16), s_b.astype(BF16))
        s_b = cd_b * s_b + kv_b[c]
    for c in range(nc):
        o = outs[c]
        mu = jnp.mean(o, axis=-1, keepdims=True)
        var = jnp.mean(jnp.square(o - mu), axis=-1, keepdims=True)
        o = ((o - mu) * lax.rsqrt(var + EPS)) * gn_ref[:, cols]
        g = chunk(g_ref, c)
        o_ref[c * C:(c + 1) * C, cols] = ((g * jax.nn.sigmoid(g)) * o).astype(BF16)
    return s_f, s_b


def _ret_kernel(lg_ref, q_ref, k_ref, v_ref, g_ref, gn_ref, *rest, seq_len, heads, has_state):
    if has_state:
        s0_ref, o_ref = rest
    else:
        o_ref, sfin_ref = rest
    hg = pl.program_id(1)
    for j in range(heads):
        hd = hg * heads + j
        cols = slice(j * RET_DK, (j + 1) * RET_DK)
        if has_state:
            s0_f, s0_b = s0_ref[0, j], s0_ref[1, j]
        else:
            s0_f = s0_b = jnp.zeros((RET_DK, RET_DV), F32)
        s_f, s_b = _ret_head(lg_ref[0, hd], lg_ref[1, hd], q_ref, k_ref, v_ref, g_ref, gn_ref, o_ref,
                             s0_f, s0_b, cols, seq_len)
        if not has_state:
            sfin_ref[0, j] = s_f
            sfin_ref[1, j] = s_b


def _retention(proj, lg, gn_g, s0, n_seq, seq_len, blk0, heads):
    H = RET_HEADS
    width = heads * RET_DK

    def col_spec(off):
        first = (S5_WIDTH + off * RET_WIDTH) // width
        return pl.BlockSpec((seq_len, width), lambda b, hg: (blk0 + b, first + hg))

    state_spec = pl.BlockSpec((None, 2, heads, RET_DK, RET_DV), lambda b, hg: (b, 0, hg, 0, 0))
    in_specs = [
        pl.BlockSpec(memory_space=pltpu.SMEM),
        col_spec(0), col_spec(1), col_spec(2), col_spec(3),
        pl.BlockSpec((1, width), lambda b, hg: (0, hg)),
    ]
    args = [lg, proj, proj, proj, proj, gn_g.reshape(1, RET_WIDTH)]
    out_specs = [pl.BlockSpec((seq_len, width), lambda b, hg: (b, hg))]
    out_shape = [jax.ShapeDtypeStruct((n_seq * seq_len, RET_WIDTH), BF16)]
    if s0 is not None:
        in_specs.append(state_spec)
        args.append(s0)
    else:
        out_specs.append(state_spec)
        out_shape.append(jax.ShapeDtypeStruct((n_seq, 2, H, RET_DK, RET_DV), F32))
    return pl.pallas_call(
        functools.partial(_ret_kernel, seq_len=seq_len, heads=heads, has_state=s0 is not None),
        grid=(n_seq, H // heads),
        in_specs=in_specs,
        out_specs=out_specs,
        out_shape=out_shape,
        compiler_params=_cparams(("parallel", "parallel")),
        name="retention_dec" if s0 is not None else "retention_ctx",
    )(*args)


def _shift_rows(x, d, ridx):
    n = x.shape[0]
    if d == 0:
        return x
    rolled = pltpu.roll(x, d % n, axis=0)
    valid = (ridx >= d) if d > 0 else (ridx < n + d)
    return jnp.where(valid, rolled, 0.0)


def _pool_kernel(u_ref, w_ref, sc_ref, o_ref, *, seq_len):
    L = seq_len
    ridx = lax.broadcasted_iota(jnp.int32, (L, POOL_GC), 0)
    for gi, w in enumerate(POOL_WINDOWS):
        cols = slice(gi * POOL_GC, (gi + 1) * POOL_GC)
        x = u_ref[:, cols]
        half = w // 2
        back, ahead = x, x
        span = 1
        while span < half:
            back = back + _shift_rows(back, span, ridx)
            ahead = ahead + _shift_rows(ahead, -span, ridx)
            span *= 2
        win = _shift_rows(back, 1, ridx) + ahead
        lo = jnp.maximum(ridx - half, 0)
        hi = jnp.minimum(ridx + half - 1, L - 1)
        mean = win / (hi - lo + 1).astype(F32)
        mixed = _dot((mean - x).astype(BF16), w_ref[gi].astype(BF16))
        o_ref[:, cols] = (mixed * sc_ref[:, cols]).astype(BF16)


def _pool(proj, w_pool, o, pool_scale, n_seq, seq_len, blk0):
    return pl.pallas_call(
        functools.partial(_pool_kernel, seq_len=seq_len),
        grid=(n_seq,),
        in_specs=[
            pl.BlockSpec((seq_len, POOL_WIDTH), lambda b: (blk0 + b, 0)),
            pl.BlockSpec((None, POOL_GROUPS, POOL_GC, POOL_GC), lambda b: (o, 0, 0, 0)),
            pl.BlockSpec((1, POOL_WIDTH), lambda b: (0, 0)),
        ],
        out_specs=pl.BlockSpec((seq_len, POOL_WIDTH), lambda b: (b, 0)),
        out_shape=jax.ShapeDtypeStruct((n_seq * seq_len, POOL_WIDTH), BF16),
        compiler_params=_cparams(("parallel",)),
        name="pool",
    )(proj, w_pool, pool_scale.reshape(1, POOL_WIDTH))


def _softmax_pv(s, v, sink):
    m = jnp.maximum(jnp.max(s, axis=-1, keepdims=True), sink)
    p = jnp.exp(s - m)
    denom = jnp.sum(p, axis=-1, keepdims=True) + jnp.exp(sink - m)
    return _dot(p.astype(BF16), v) / denom


def _qk(q, k):
    return lax.dot_general(q, k, (((1,), (1,)), ((), ())), preferred_element_type=F32) * (ATT_HD ** -0.5)


def _ctx_att_kernel(sink_ref, q_ref, k_ref, v_ref, o_ref, ko_ref, vo_ref):
    ko_ref[...] = k_ref[...]
    vo_ref[...] = v_ref[...]
    for kv in range(ATT_KV):
        kv_cols = slice(kv * ATT_HD, (kv + 1) * ATT_HD)
        k = k_ref[:, kv_cols].astype(BF16)
        v = v_ref[:, kv_cols].astype(BF16)
        for g in range(ATT_GROUP):
            head = kv * ATT_GROUP + g
            cols = slice(head * ATT_HD, (head + 1) * ATT_HD)
            s = _qk(q_ref[:, cols].astype(BF16), k)
            o_ref[:, cols] = _softmax_pv(s, v, sink_ref[head]).astype(BF16)


def _ctx_attention(proj, sink):
    qw = ATT_HEADS * ATT_HD
    kvw = ATT_KV * ATT_HD
    kb = (POOL_WIDTH + qw) // kvw
    return pl.pallas_call(
        _ctx_att_kernel,
        grid=(BATCH,),
        in_specs=[
            pl.BlockSpec(memory_space=pltpu.SMEM),
            pl.BlockSpec((SEQ, qw), lambda b: (b, POOL_WIDTH // qw)),
            pl.BlockSpec((SEQ, kvw), lambda b: (b, kb)),
            pl.BlockSpec((SEQ, kvw), lambda b: (b, kb + 1)),
        ],
        out_specs=[pl.BlockSpec((SEQ, qw), lambda b: (b, 0)),
                   pl.BlockSpec((SEQ, kvw), lambda b: (b, 0)),
                   pl.BlockSpec((SEQ, kvw), lambda b: (b, 0))],
        out_shape=[jax.ShapeDtypeStruct((N_CTX_TOK, qw), BF16),
                   jax.ShapeDtypeStruct((N_CTX_TOK, kvw), F32),
                   jax.ShapeDtypeStruct((N_CTX_TOK, kvw), F32)],
        compiler_params=_cparams(("parallel",)),
        name="attention_ctx",
    )(sink, proj, proj, proj)


def _rope(x, cos, sin):
    lane = lax.broadcasted_iota(jnp.int32, x.shape, 1)
    partner = jnp.where((lane & 63) < 32, pltpu.roll(x, 96, axis=1), pltpu.roll(x, 32, axis=1))
    return x * cos + partner * sin


def _dec_att_kernel(sink_ref, q_ref, k_ref, v_ref, kc_ref, vc_ref, cos_ref, sin_ref, o_ref, kr_ref):
    kv = pl.program_id(1)
    L, B = DEC_SEQ, ATT_BLOCK
    nb = L // B
    kr_ref[...] = _rope(k_ref[...], cos_ref[...], sin_ref[...]).astype(BF16)
    kc = kc_ref[...].astype(BF16)
    vc = vc_ref[...].astype(BF16)
    for i in range(nb):
        lo, hi = max(i - 1, 0) * B, min(i + 2, nb) * B
        rows = slice(i * B, (i + 1) * B)
        kwin = kr_ref[lo:hi, :]
        vals = jnp.concatenate([v_ref[lo:hi, :].astype(BF16), vc], axis=0)
        qpos = i * B + lax.broadcasted_iota(jnp.int32, (B, hi - lo), 0)
        kpos = lo + lax.broadcasted_iota(jnp.int32, (B, hi - lo), 1)
        band = jnp.abs(kpos - qpos) <= ATT_WIN
        for g in range(ATT_GROUP):
            cols = slice(g * ATT_HD, (g + 1) * ATT_HD)
            q = _rope(q_ref[rows, cols], cos_ref[rows, :], sin_ref[rows, :]).astype(BF16)
            s = jnp.concatenate([jnp.where(band, _qk(q, kwin), NEG), _qk(q, kc)], axis=1)
            o_ref[rows, cols] = _softmax_pv(s, vals, sink_ref[kv * ATT_GROUP + g]).astype(BF16)


def _rope_tables():
    half = ATT_HD // 2
    nfreq = half // 2
    freqs = ROPE_BASE ** (-jnp.arange(nfreq, dtype=F32) / nfreq)
    t = jnp.arange(DEC_SEQ)
    pos = jnp.stack([t // GRID_W, t % GRID_W], axis=1).astype(F32)
    ang = pos[:, :, None] * freqs[None, None, :]
    cos = jnp.repeat(jnp.cos(ang), 2, axis=1).reshape(DEC_SEQ, ATT_HD)
    sin = jnp.sin(ang)
    sin = jnp.stack([-sin, sin], axis=2).reshape(DEC_SEQ, ATT_HD)
    return cos, sin


def _dec_attention(proj, cache_k, cache_v, sink, cos, sin, o):
    qw = ATT_GROUP * ATT_HD
    qb = POOL_WIDTH // qw
    kb = (POOL_WIDTH + ATT_HEADS * ATT_HD) // ATT_HD
    blk0 = N_CTX_TOK // DEC_SEQ
    ck = cache_k.reshape(DEC_BATCH, N_ODD, PAST_LEN, ATT_KV * ATT_HD)
    cv = cache_v.reshape(DEC_BATCH, N_ODD, PAST_LEN, ATT_KV * ATT_HD)
    cache_spec = pl.BlockSpec((None, None, PAST_LEN, ATT_HD), lambda b, kv: (b, o, 0, kv))
    table_spec = pl.BlockSpec((DEC_SEQ, ATT_HD), lambda b, kv: (0, 0))
    return pl.pallas_call(
        _dec_att_kernel,
        grid=(DEC_BATCH, ATT_KV),
        in_specs=[
            pl.BlockSpec(memory_space=pltpu.SMEM),
            pl.BlockSpec((DEC_SEQ, qw), lambda b, kv: (blk0 + b, qb + kv)),
            pl.BlockSpec((DEC_SEQ, ATT_HD), lambda b, kv: (blk0 + b, kb + kv)),
            pl.BlockSpec((DEC_SEQ, ATT_HD), lambda b, kv: (blk0 + b, kb + ATT_KV + kv)),
            cache_spec, cache_spec, table_spec, table_spec,
        ],
        out_specs=pl.BlockSpec((DEC_SEQ, qw), lambda b, kv: (b, kv)),
        out_shape=jax.ShapeDtypeStruct((N_DEC_TOK, ATT_HEADS * ATT_HD), BF16),
        scratch_shapes=[pltpu.VMEM((DEC_SEQ, ATT_HD), BF16)],
        compiler_params=_cparams(("parallel", "parallel")),
        name="attention_dec",
    )(sink, proj, proj, proj, ck, cv, cos, sin)


def kernel(x_prompt, x_sample, c, state_s5_re, state_s5_im, state_ret, cache_k, cache_v, c_ctx, w_mod, b_mod, norm_g, ffn1_gate, ffn1_up, ffn1_down, ffn2_gate, ffn2_up, ffn2_down, even_w_in, even_w_out, s5_lam_re, s5_lam_im, s5_log_dt, s5_b_re, s5_b_im, s5_c_re, s5_c_im, s5_d, s5_glu_w, s5_glu_b, ret_decay_logit, ret_gn_g, odd_w_in, odd_w_out, pool_w, pool_scale, att_sink):
    x = jnp.concatenate([x_prompt.reshape(N_CTX_TOK, D_MODEL), x_sample.reshape(N_DEC_TOK, D_MODEL)], axis=0)
    c8 = jnp.concatenate([c_ctx[None, :], c, jnp.zeros((MOD_ROWS - 1 - DEC_BATCH, D_MODEL), F32)], axis=0)
    mod5 = _modulation(c8, w_mod, b_mod).reshape(DEPTH, MOD_ROWS, N_MOD, 1, D_MODEL)
    norm4 = norm_g.reshape(DEPTH, 6, 1, D_MODEL)
    cos, sin = _rope_tables()
    dec_blk = N_CTX_TOK // DEC_SEQ
    out_re, out_im, out_ret, out_k, out_v = [], [], [], [], []
    for l in range(DEPTH):
        x = _ffn(x, mod5, norm4, ffn1_gate, ffn1_up, ffn1_down, l, 0, 0, 1)
        if l % 2 == 0:
            e = l // 2
            proj = _proj_in(x, mod5, norm4, even_w_in, e, l)
            ops = _s5_operators(s5_lam_re[e], s5_lam_im[e], s5_log_dt[e], s5_b_re[e], s5_b_im[e],
                                s5_c_re[e], s5_c_im[e])
            h0 = jnp.stack([state_s5_re[:, e], state_s5_im[:, e]], axis=2)
            h0 = jnp.transpose(h0, (3, 1, 2, 4, 0)).reshape(S5_GROUPS, 4 * S5_P, DEC_BATCH)
            h0 = jnp.repeat(h0, S5_DEC_CHUNKS, axis=2)
            y_a, s_re, s_im = _s5_mixer(proj, ops, h0, s5_d[e], s5_glu_w, s5_glu_b[e], e)
            lg = jax.nn.log_sigmoid(ret_decay_logit[e].astype(F32))
            yb_ctx, s_ret = _retention(proj, lg, ret_gn_g[e], None, BATCH, SEQ, 0, RET_HEADS)
            (yb_dec,) = _retention(proj, lg, ret_gn_g[e], state_ret[:, e], DEC_BATCH, DEC_SEQ, dec_blk, 2)
            y_b = (yb_ctx, yb_dec)
            out_re.append(s_re)
            out_im.append(s_im)
            out_ret.append(s_ret)
            w_out, w_idx = even_w_out, e
        else:
            o = l // 2
            proj = _proj_in(x, mod5, norm4, odd_w_in, o, l)
            y_a = (_pool(proj, pool_w, o, pool_scale[o], BATCH, SEQ, 0),
                   _pool(proj, pool_w, o, pool_scale[o], DEC_BATCH, DEC_SEQ, dec_blk))
            sink = att_sink[o].astype(F32)
            yd_ctx, k_ctx, v_ctx = _ctx_attention(proj, sink)
            y_b = (yd_ctx, _dec_attention(proj, cache_k, cache_v, sink, cos, sin, o))
            out_k.append(k_ctx.reshape(BATCH, SEQ, ATT_KV, ATT_HD))
            out_v.append(v_ctx.reshape(BATCH, SEQ, ATT_KV, ATT_HD))
            w_out, w_idx = odd_w_out, o
        x = _proj_out(x, y_a, y_b, mod5, norm4, w_out, w_idx, l)
        x = _ffn(x, mod5, norm4, ffn2_gate, ffn2_up, ffn2_down, l, 2, 4, 5)
    y_prompt = x[:N_CTX_TOK].reshape(BATCH, SEQ, D_MODEL)
    y_sample = x[N_CTX_TOK:].reshape(DEC_BATCH, DEC_SEQ, D_MODEL)
    return (y_prompt, y_sample, jnp.stack(out_re, axis=1), jnp.stack(out_im, axis=1),
            jnp.stack(out_ret, axis=1), jnp.stack(out_k, axis=1), jnp.stack(out_v, axis=1))
```

```python
import functools
import math

import jax
import jax.numpy as jnp
from jax import lax
from jax.experimental import pallas as pl
from jax.experimental.pallas import tpu as pltpu

F32 = jnp.float32
BF16 = jnp.bfloat16

D_MODEL = 2048
BATCH = 16
SEQ = 256
DEPTH = 4
DEC_BATCH = 2
DEC_SEQ = 1024
PAST_LEN = 256
GRID_W = 64
N_EVEN = (DEPTH + 1) // 2
N_ODD = DEPTH // 2
N_MOD = 9
FFN_HIDDEN = 5632
EPS = 1e-6
NEG = -1e30

S5_WIDTH = D_MODEL // 2
S5_GROUP = 16
S5_GROUPS = S5_WIDTH // S5_GROUP
S5_P = 64
RET_HEADS = 8
RET_DK = (D_MODEL // 2) // RET_HEADS
RET_DV = RET_DK
RET_WIDTH = RET_HEADS * RET_DV
RET_CHUNK = 128
EVEN_IN = S5_WIDTH + 4 * RET_WIDTH

POOL_WIDTH = D_MODEL // 2
POOL_WINDOWS = (2, 4, 8, 16)
POOL_GROUPS = 4
POOL_GC = POOL_WIDTH // POOL_GROUPS
ATT_HEADS = 8
ATT_KV = 2
ATT_HD = (D_MODEL // 2) // ATT_HEADS
ATT_GROUP = ATT_HEADS // ATT_KV
ATT_WIN = 128
ATT_BLOCK = 128
ODD_IN = POOL_WIDTH + (ATT_HEADS + 2 * ATT_KV) * ATT_HD
ROPE_BASE = 10000.0

N_CTX_TOK = BATCH * SEQ
N_DEC_TOK = DEC_BATCH * DEC_SEQ
N_TOK = N_CTX_TOK + N_DEC_TOK
MOD_ROWS = 8

S5_T = 16
S5_LANES = S5_T * S5_GROUP
S5_CTX_CHUNKS = SEQ // S5_T
S5_DEC_CHUNKS = DEC_SEQ // S5_T
S5_CTX_ROWS = BATCH * S5_CTX_CHUNKS
S5_DEC_ROWS = DEC_BATCH * S5_DEC_CHUNKS
S5_ROWS = S5_CTX_ROWS + S5_DEC_ROWS
S5_NPOW = 6

VMEM_LIMIT = 60 * 1024 * 1024


def _cparams(sem):
    return pltpu.CompilerParams(dimension_semantics=sem, vmem_limit_bytes=VMEM_LIMIT)


def _dot(a, b):
    return jnp.dot(a, b, preferred_element_type=F32)


def _rms(x, g):
    return (x * lax.rsqrt(jnp.mean(x * x, axis=-1, keepdims=True) + EPS)) * g


def _mod_row(i, tm):
    start = i * tm
    return (start >= N_CTX_TOK).astype(jnp.int32) + (start >= N_CTX_TOK + DEC_SEQ).astype(jnp.int32)


def _mod_kernel(c_ref, w_ref, b_ref, o_ref):
    c = c_ref[...]
    s = (c * jax.nn.sigmoid(c)).astype(BF16)
    o_ref[...] = _dot(s, w_ref[...].astype(BF16)) + b_ref[...]


def _modulation(c8, w_mod, b_mod):
    tn = 1024
    n = N_MOD * D_MODEL
    return pl.pallas_call(
        _mod_kernel,
        grid=(DEPTH, n // tn),
        in_specs=[
            pl.BlockSpec((MOD_ROWS, D_MODEL), lambda l, j: (0, 0)),
            pl.BlockSpec((None, D_MODEL, tn), lambda l, j: (l, 0, j)),
            pl.BlockSpec((None, 1, tn), lambda l, j: (l, 0, j)),
        ],
        out_specs=pl.BlockSpec((None, MOD_ROWS, tn), lambda l, j: (l, 0, j)),
        out_shape=jax.ShapeDtypeStruct((DEPTH, MOD_ROWS, n), F32),
        compiler_params=_cparams(("arbitrary", "arbitrary")),
        name="modulation",
    )(c8, w_mod, b_mod.reshape(DEPTH, 1, n))


def _mod_spec(l, j, tm, tile0=0):
    return pl.BlockSpec((None, None, None, 1, D_MODEL),
                        lambda i, *_: (l, _mod_row(tile0 + i, tm), j, 0, 0))


def _gain_spec(l, j):
    return pl.BlockSpec((None, None, 1, D_MODEL), lambda i, *_: (l, j, 0, 0))


FFN_TM = 1024
FFN_TF = 256
SLAB_ROWS = 256


def _adaln_in(x, g_ref, sc_ref, sh_ref):
    return (_rms(x, g_ref[...]) * (1.0 + sc_ref[...]) + sh_ref[...]).astype(BF16)


def _ffn_kernel(x_ref, sh_ref, sc_ref, gate_ref, gin_ref, gout_ref, wg_ref, wu_ref, wd_ref,
                o_ref, h_ref, *, nf, tm):
    f = pl.program_id(1)
    slabs = [slice(r0, r0 + SLAB_ROWS) for r0 in range(0, tm, SLAB_ROWS)]

    def weights():
        return wg_ref[...].astype(BF16), wu_ref[...].astype(BF16), wd_ref[...].astype(BF16)

    def swiglu(h, w):
        g = _dot(h, w[0])
        u = _dot(h, w[1])
        a = ((g * jax.nn.sigmoid(g)) * u).astype(BF16)
        return _dot(a, w[2])

    @pl.when(f == 0)
    def _():
        w = weights()
        for rows in slabs:
            h = _adaln_in(x_ref[rows, :], gin_ref, sc_ref, sh_ref)
            h_ref[rows, :] = h
            o_ref[rows, :] = swiglu(h, w)

    @pl.when(jnp.logical_and(f > 0, f < nf - 1))
    def _():
        o_ref[...] += swiglu(h_ref[...], weights())

    @pl.when(f == nf - 1)
    def _():
        w = weights()
        for rows in slabs:
            y = o_ref[rows, :] + swiglu(h_ref[rows, :], w)
            o_ref[rows, :] = x_ref[rows, :] + (0.5 * gate_ref[...]) * _rms(y, gout_ref[...])


def _ffn(x, mod5, norm4, w_gate, w_up, w_down, l, j, g_in, g_out, rows=(0, N_TOK)):
    tm, tf = FFN_TM, FFN_TF
    nf = FFN_HIDDEN // tf
    tile0, n_tiles = rows[0] // tm, (rows[1] - rows[0]) // tm
    return pl.pallas_call(
        functools.partial(_ffn_kernel, nf=nf, tm=tm),
        grid=(n_tiles, nf),
        in_specs=[
            pl.BlockSpec((tm, D_MODEL), lambda i, f: (tile0 + i, 0)),
            _mod_spec(l, 3 * j, tm, tile0), _mod_spec(l, 3 * j + 1, tm, tile0), _mod_spec(l, 3 * j + 2, tm, tile0),
            _gain_spec(l, g_in), _gain_spec(l, g_out),
            pl.BlockSpec((None, D_MODEL, tf), lambda i, f: (l, 0, f)),
            pl.BlockSpec((None, D_MODEL, tf), lambda i, f: (l, 0, f)),
            pl.BlockSpec((None, tf, D_MODEL), lambda i, f: (l, f, 0)),
        ],
        out_specs=pl.BlockSpec((tm, D_MODEL), lambda i, f: (i, 0)),
        out_shape=jax.ShapeDtypeStruct((n_tiles * tm, D_MODEL), F32),
        scratch_shapes=[pltpu.VMEM((tm, D_MODEL), BF16)],
        compiler_params=_cparams(("parallel", "arbitrary")),
        name="ffn",
    )(x, mod5, mod5, mod5, norm4, norm4, w_gate, w_up, w_down)


PROJ_TM = 1024
PROJ_TN = 1280


def _proj_in_kernel(x_ref, sh_ref, sc_ref, gin_ref, w_ref, o_ref, h_ref, *, tm):
    k = pl.program_id(1)

    @pl.when(k == 0)
    def _():
        w = w_ref[...].astype(BF16)
        for r0 in range(0, tm, SLAB_ROWS):
            rows = slice(r0, r0 + SLAB_ROWS)
            h = _adaln_in(x_ref[rows, :], gin_ref, sc_ref, sh_ref)
            h_ref[rows, :] = h
            o_ref[rows, :] = _dot(h, w)

    @pl.when(k > 0)
    def _():
        o_ref[...] = _dot(h_ref[...], w_ref[...].astype(BF16))


def _proj_in(x, mod5, norm4, w_in, e, l):
    tm, tn = PROJ_TM, PROJ_TN
    n = w_in.shape[-1]
    return pl.pallas_call(
        functools.partial(_proj_in_kernel, tm=tm),
        grid=(N_TOK // tm, n // tn),
        in_specs=[
            pl.BlockSpec((tm, D_MODEL), lambda i, k: (i, 0)),
            _mod_spec(l, 3, tm), _mod_spec(l, 4, tm),
            _gain_spec(l, 2),
            pl.BlockSpec((None, D_MODEL, tn), lambda i, k: (e, 0, k)),
        ],
        out_specs=pl.BlockSpec((tm, tn), lambda i, k: (i, k)),
        out_shape=jax.ShapeDtypeStruct((N_TOK, n), F32),
        scratch_shapes=[pltpu.VMEM((tm, D_MODEL), BF16)],
        compiler_params=_cparams(("parallel", "arbitrary")),
        name="proj_in",
    )(x, mod5, mod5, norm4, w_in)


OUT_TM = 512


OUT_ROWS = 256


def _proj_out_kernel(x_ref, yac_ref, yad_ref, ybc_ref, ybd_ref, gate_ref, gout_ref, w_ref, o_ref, *, tm):
    def run(ya_ref, yb_ref):
        w = w_ref[...].astype(BF16)
        for r0 in range(0, tm, OUT_ROWS):
            rows = slice(r0, r0 + OUT_ROWS)
            y = _dot(jnp.concatenate([ya_ref[rows, :], yb_ref[rows, :]], axis=1), w)
            o_ref[rows, :] = x_ref[rows, :] + gate_ref[...] * _rms(y, gout_ref[...])

    is_ctx = pl.program_id(0) < N_CTX_TOK // tm
    pl.when(is_ctx)(lambda: run(yac_ref, ybc_ref))
    pl.when(jnp.logical_not(is_ctx))(lambda: run(yad_ref, ybd_ref))


def _proj_out(x, ya, yb, mod5, norm4, w_out, e, l):
    tm = OUT_TM
    half = D_MODEL // 2
    nct = N_CTX_TOK // tm

    def split_specs(y):
        ctx_spec = pl.BlockSpec((tm, half), lambda i: (jnp.minimum(i, nct - 1), 0))
        if isinstance(y, tuple):
            return y, (ctx_spec, pl.BlockSpec((tm, half), lambda i: (jnp.maximum(i - nct, 0), 0)))
        return (y, y), (ctx_spec, pl.BlockSpec((tm, half), lambda i: (jnp.maximum(i, nct), 0)))

    (ya_c, ya_d), (sa_c, sa_d) = split_specs(ya)
    (yb_c, yb_d), (sb_c, sb_d) = split_specs(yb)
    return pl.pallas_call(
        functools.partial(_proj_out_kernel, tm=tm),
        grid=(N_TOK // tm,),
        in_specs=[
            pl.BlockSpec((tm, D_MODEL), lambda i: (i, 0)),
            sa_c, sa_d, sb_c, sb_d,
            _mod_spec(l, 5, tm),
            _gain_spec(l, 3),
            pl.BlockSpec((None, D_MODEL, D_MODEL), lambda i: (e, 0, 0), pipeline_mode=pl.Buffered(1)),
        ],
        out_specs=pl.BlockSpec((tm, D_MODEL), lambda i: (i, 0)),
        out_shape=jax.ShapeDtypeStruct((N_TOK, D_MODEL), F32),
        compiler_params=_cparams(("parallel",)),
        name="proj_out",
    )(x, ya_c, ya_d, yb_c, yb_d, mod5, norm4, w_out)


def _s5_operators(lam_re, lam_im, log_dt, b_re, b_im, c_re, c_im):
    T, P, G = S5_T, S5_P, S5_GROUPS
    lam = lax.complex(jnp.minimum(lam_re.astype(F32), -1e-4), lam_im.astype(F32))
    ldt = lam * jnp.exp(log_dt.astype(F32))[..., None]
    a = jnp.exp(ldt)
    b_t = jnp.swapaxes(lax.complex(b_re.astype(F32), b_im.astype(F32)), 2, 3)
    bbar = ((a - 1.0) / lam)[:, :, None, :] * b_t
    c = lax.complex(c_re.astype(F32), c_im.astype(F32))
    s = jnp.arange(T, dtype=F32)
    full = jnp.full((1,), T, F32)
    n_pw = 3 * T + 1
    n_f = jnp.concatenate([T - 1.0 - s, -1.0 - s, s + 1.0, full])
    n_b = jnp.concatenate([s, s - T, T - s, full])
    zf = jnp.exp(n_f[:, None] * ldt[0].reshape(1, G * P))
    zb = jnp.exp(n_b[:, None] * ldt[1].reshape(1, G * P))
    pw = lax.optimization_barrier(jnp.stack([jnp.real(zf), jnp.imag(zf), jnp.real(zb), jnp.imag(zb)], axis=1))
    pw = jnp.transpose(pw.reshape(n_pw, 4, G, P), (2, 0, 1, 3)).reshape(G, n_pw, 4 * P)

    def lanes(zf, zb):
        return jnp.concatenate([jnp.real(zf), jnp.imag(zf), jnp.real(zb), jnp.imag(zb)], axis=2)

    gap = jnp.zeros((G, S5_F_B - n_pw, 4 * P), F32)
    tail = jnp.zeros((G, 128 - S5_F_C - S5_GROUP, 4 * P), F32)
    return jnp.concatenate([pw, gap, lanes(bbar[0], bbar[1]), lanes(c[0], c[1]), tail], axis=1)


S5_F_INC, S5_F_KIN, S5_F_OUT, S5_F_A, S5_F_B, S5_F_C = 0, S5_T, 2 * S5_T, 3 * S5_T, 4 * S5_T, 5 * S5_T


def _cmul_rows(x, ar, ai):
    h = x.shape[0] // 2
    xr, xi = x[:h], x[h:]
    return jnp.concatenate([xr * ar - xi * ai, xr * ai + xi * ar], axis=0)


def _s5_lane_scan(inc, a_rows, h0, lc):
    P = S5_P
    cidx = lax.broadcasted_iota(jnp.int32, (2 * P, 128), 1) & (lc - 1)
    fwd, bwd = inc[:2 * P], inc[2 * P:]
    arf, aif, arb, aib = a_rows[:P], a_rows[P:2 * P], a_rows[2 * P:3 * P], a_rows[3 * P:]
    if h0 is not None:
        fwd = fwd + jnp.where(cidx == 0, _cmul_rows(h0[:2 * P], arf, aif), 0.0)
        bwd = bwd + jnp.where(cidx == lc - 1, _cmul_rows(h0[2 * P:], arb, aib), 0.0)
    d = 1
    while d < lc:
        sh = jnp.where(cidx >= d, pltpu.roll(fwd, d, axis=1), 0.0)
        fwd = fwd + _cmul_rows(sh, arf, aif)
        sh = jnp.where(cidx < lc - d, pltpu.roll(bwd, 128 - d, axis=1), 0.0)
        bwd = bwd + _cmul_rows(sh, arb, aib)
        d *= 2
        if d < lc:
            arf, aif = arf * arf - aif * aif, 2.0 * arf * aif
            arb, aib = arb * arb - aib * aib, 2.0 * arb * aib
    first_f = 0.0 if h0 is None else h0[:2 * P]
    first_b = 0.0 if h0 is None else h0[2 * P:]
    start_f = jnp.where(cidx >= 1, pltpu.roll(fwd, 1, axis=1), first_f)
    start_b = jnp.where(cidx < lc - 1, pltpu.roll(bwd, 127, axis=1), first_b)
    return jnp.concatenate([start_f, start_b], axis=0), jnp.concatenate([fwd, bwd], axis=0)


def _dot_split(a, b):
    a_hi = a.astype(BF16)
    b_hi = b.astype(BF16)
    a_lo = (a - a_hi.astype(F32)).astype(BF16)
    b_lo = (b - b_hi.astype(F32)).astype(BF16)
    return _dot(a_hi, b_hi) + (_dot(a_hi, b_lo) + _dot(a_lo, b_hi))


S5_SLAB = 128 // S5_GROUP


def _split3(x):
    hi = x.astype(BF16)
    r = x - hi.astype(F32)
    mid = r.astype(BF16)
    return hi, mid, (r - mid.astype(F32)).astype(BF16)


def _cprod_rows(a, b, conj_sign):
    h = a.shape[0] // 2
    ar, ai, br, bi = a[:h], a[h:], b[:h], b[h:]
    return jnp.concatenate([ar * br - ai * bi, conj_sign * (ar * bi + ai * br)], axis=0)


def _s5_kernel(u_ref, f_ref, h0_ref, y_ref, fin_ref, xt_ref, yt_ref):
    T, K, P, R = S5_T, S5_GROUP, S5_P, S5_ROWS
    F = S5_LANES
    for s in range(T):
        ut = u_ref[pl.ds(s, R, stride=T), :].T
        for gg in range(S5_SLAB):
            xt_ref[gg, s * K:(s + 1) * K, :] = ut[gg * K:(gg + 1) * K, :].astype(BF16)
    shift = K.bit_length() - 1
    tok_out = lax.broadcasted_iota(jnp.int32, (F, F), 0) >> shift
    tok_in = lax.broadcasted_iota(jnp.int32, (F, F), 1) >> shift
    lane = lax.broadcasted_iota(jnp.int32, (128, F), 0)
    feat = lax.broadcasted_iota(jnp.int32, (128, F), 1)
    tok, ch = feat >> shift, feat & (K - 1)

    def spread(field, index):
        return jnp.where(lane == field + index, 1.0, 0.0).astype(BF16)

    e_inc, e_kin, e_out = spread(S5_F_INC, tok), spread(S5_F_KIN, tok), spread(S5_F_OUT, tok)
    e_b, e_c = spread(S5_F_B, ch), spread(S5_F_C, ch)
    lane_a = lax.broadcasted_iota(jnp.int32, (128, 128), 0)
    e_a = jnp.where(lane_a == S5_F_A, 1.0, 0.0).astype(BF16)
    seq = lax.broadcasted_iota(jnp.int32, (128, 128), 1)
    nseq = 128 // S5_CTX_CHUNKS
    sel_last = [jnp.where((lane_a == (seq - o) * S5_CTX_CHUNKS + S5_CTX_CHUNKS - 1) & (seq >= o) & (seq < o + nseq),
                          1.0, 0.0).astype(BF16) for o in (0, nseq)]
    sel_first = [jnp.where((lane_a == (seq - o) * S5_CTX_CHUNKS) & (seq >= o) & (seq < o + nseq),
                           1.0, 0.0).astype(BF16) for o in (0, nseq)]

    def group(gg, carry):
        hi, mid, lo = _split3(f_ref[gg].T)

        def expand(e, exact=False):
            out = _dot(hi, e) + _dot(mid, e)
            return out + _dot(lo, e) if exact else out

        def cprod(pw, other, conj_sign):
            return jnp.concatenate([_cprod_rows(pw[:2 * P], other[:2 * P], conj_sign),
                                    _cprod_rows(pw[2 * P:], other[2 * P:], conj_sign)], axis=0)

        bb = expand(e_b)
        inc_op = cprod(expand(e_inc), bb, 1.0)
        kin_op = cprod(expand(e_kin), bb, 1.0)
        out_op = cprod(expand(e_out), expand(e_c), -1.0).T
        a_rows = expand(e_a, exact=True)
        resp_f = _dot_split(out_op[:, :2 * P], kin_op[:2 * P])
        resp_b = _dot_split(out_op[:, 2 * P:], kin_op[2 * P:])
        resp = jnp.where(tok_out >= tok_in, resp_f, 0.0) + jnp.where(tok_out <= tok_in, resp_b, 0.0)
        ops = jnp.concatenate([resp, inc_op], axis=0).astype(BF16)
        z = _dot(ops, xt_ref[gg])
        lanes = S5_CTX_ROWS // 2
        st_a, sc_a = _s5_lane_scan(z[F:, :lanes], a_rows, None, S5_CTX_CHUNKS)
        st_b, sc_b = _s5_lane_scan(z[F:, lanes:2 * lanes], a_rows, None, S5_CTX_CHUNKS)
        st_d, _ = _s5_lane_scan(z[F:, 2 * lanes:], a_rows, h0_ref[gg], S5_DEC_CHUNKS)
        start = jnp.concatenate([st_a, st_b, st_d], axis=1).astype(BF16)
        yt_ref[gg] = z[:F] + _dot(out_op.astype(BF16), start)
        fin_f = sum(_dot(t, sel_last[0]) for t in _split3(sc_a[:2 * P])) \
            + sum(_dot(t, sel_last[1]) for t in _split3(sc_b[:2 * P]))
        fin_b = sum(_dot(t, sel_first[0]) for t in _split3(sc_a[2 * P:])) \
            + sum(_dot(t, sel_first[1]) for t in _split3(sc_b[2 * P:]))
        fin_ref[gg] = jnp.concatenate([fin_f, fin_b], axis=0)
        return carry

    lax.fori_loop(0, S5_SLAB, group, 0)
    for i in range(T):
        yt = jnp.concatenate([yt_ref[gg, i * K:(i + 1) * K, :] for gg in range(S5_SLAB)], axis=0)
        y_ref[pl.ds(i, R, stride=T), :] = yt.T


def _s5_scan(proj, factors, h0):
    G, F, P4 = S5_GROUPS, S5_LANES, 4 * S5_P
    assert S5_CTX_ROWS == 256 and S5_DEC_ROWS == 128 and F == 256 and P4 == 256 and S5_T == S5_GROUP
    op_spec = pl.BlockSpec((S5_SLAB, P4, 128), lambda j: (j, 0, 0))
    factor_spec = pl.BlockSpec((S5_SLAB, 128, P4), lambda j: (j, 0, 0))
    return pl.pallas_call(
        _s5_kernel,
        grid=(G // S5_SLAB,),
        in_specs=[pl.BlockSpec((N_TOK, 128), lambda j: (0, j)), factor_spec, op_spec],
        out_specs=[pl.BlockSpec((N_TOK, 128), lambda j: (0, j)), op_spec],
        out_shape=[
            jax.ShapeDtypeStruct((N_TOK, S5_WIDTH), F32),
            jax.ShapeDtypeStruct((G, P4, 128), F32),
        ],
        scratch_shapes=[pltpu.VMEM((S5_SLAB, F, S5_ROWS), BF16), pltpu.VMEM((S5_SLAB, F, S5_ROWS), F32)],
        compiler_params=_cparams(("parallel",)),
        name="s5_scan",
    )(proj, factors, h0)


S5_POST_TM = 512


def _s5_post_kernel(y_ref, u_ref, d_ref, w_ref, b_ref, o_ref):
    y = y_ref[...] + d_ref[...] * u_ref[...]
    z = 0.5 * y * (1.0 + jnp.tanh(math.sqrt(2.0 / math.pi) * (y + 0.044715 * (y * y * y))))
    gate = jax.nn.sigmoid(_dot(z.astype(BF16), w_ref[...].astype(BF16)) + b_ref[...])
    o_ref[...] = (z * gate).astype(BF16)


def _s5_post(y, proj, d_skip, w_glu, b_glu, e):
    tm = S5_POST_TM
    return pl.pallas_call(
        _s5_post_kernel,
        grid=(N_TOK // tm,),
        in_specs=[
            pl.BlockSpec((tm, S5_WIDTH), lambda i: (i, 0)),
            pl.BlockSpec((tm, S5_WIDTH), lambda i: (i, 0)),
            pl.BlockSpec((1, S5_WIDTH), lambda i: (0, 0)),
            pl.BlockSpec((None, S5_WIDTH, S5_WIDTH), lambda i: (e, 0, 0), pipeline_mode=pl.Buffered(1)),
            pl.BlockSpec((1, S5_WIDTH), lambda i: (0, 0)),
        ],
        out_specs=pl.BlockSpec((tm, S5_WIDTH), lambda i: (i, 0)),
        out_shape=jax.ShapeDtypeStruct((N_TOK, S5_WIDTH), BF16),
        compiler_params=_cparams(("parallel",)),
        name="s5_post",
    )(y, proj, d_skip.reshape(1, S5_WIDTH), w_glu, b_glu.reshape(1, S5_WIDTH))


def _s5_mixer(proj, ops, h0, d_skip, w_glu, b_glu, e):
    y, fin = _s5_scan(proj, ops, h0)
    y_a = _s5_post(y, proj, d_skip, w_glu, b_glu, e)
    fin = fin[:, :, :BATCH].reshape(S5_GROUPS, 2, 2, S5_P, BATCH)
    fin = jnp.transpose(fin, (4, 1, 2, 0, 3))
    return y_a, fin[:, :, 0], fin[:, :, 1]


def _ret_head(lgf, lgb, q_ref, k_ref, v_ref, g_ref, gn_ref, o_ref, s0_f, s0_b, cols, seq_len):
    C = RET_CHUNK
    nc = seq_len // C
    row = lax.broadcasted_iota(jnp.int32, (C, C), 0).astype(F32)
    col = lax.broadcasted_iota(jnp.int32, (C, C), 1).astype(F32)
    rel = row - col
    decay = (jnp.where(rel >= 0, jnp.exp(jnp.maximum(rel, 0.0) * lgf), 0.0)
             + jnp.where(rel <= 0, jnp.exp(jnp.maximum(-rel, 0.0) * lgb), 0.0))
    kdec_f = jnp.exp((C - 1.0 - row) * lgf)
    kdec_b = jnp.exp(row * lgb)
    qdec_f = jnp.exp((row + 1.0) * lgf)
    qdec_b = jnp.exp((C - row) * lgb)
    cd_f = jnp.exp(jnp.full((1, RET_DV), C * lgf, F32))
    cd_b = jnp.exp(jnp.full((1, RET_DV), C * lgb, F32))

    def chunk(ref, c):
        return ref[c * C:(c + 1) * C, cols]

    tdot = functools.partial(lax.dot_general, dimension_numbers=(((0,), (0,)), ((), ())),
                             preferred_element_type=F32)
    outs = []
    kv_f, kv_b = [], []
    for c in range(nc):
        q = chunk(q_ref, c)
        k = chunk(k_ref, c) * (RET_DK ** -0.5)
        v = chunk(v_ref, c).astype(BF16)
        inner = lax.dot_general(q.astype(BF16), k.astype(BF16), (((1,), (1,)), ((), ())),
                                preferred_element_type=F32) * decay
        outs.append(_dot(inner.astype(BF16), v))
        kv_f.append(tdot((k * kdec_f).astype(BF16), v))
        kv_b.append(tdot((k * kdec_b).astype(BF16), v))
    s_f = s0_f
    for c in range(nc):
        outs[c] = outs[c] + _dot((chunk(q_ref, c) * qdec_f).astype(BF16), s_f.astype(BF16))
        s_f = cd_f * s_f + kv_f[c]
    s_b = s0_b
    for c in reversed(range(nc)):
        outs[c] = outs[c] + _dot((chunk(q_ref, c) * qdec_b).astype(BF16), s_b.astype(BF16))
        s_b = cd_b * s_b + kv_b[c]
    for c in range(nc):
        o = outs[c]
        mu = jnp.mean(o, axis=-1, keepdims=True)
        var = jnp.mean(jnp.square(o - mu), axis=-1, keepdims=True)
        o = ((o - mu) * lax.rsqrt(var + EPS)) * gn_ref[:, cols]
        g = chunk(g_ref, c)
        o_ref[c * C:(c + 1) * C, cols] = ((g * jax.nn.sigmoid(g)) * o).astype(BF16)
    return s_f, s_b


def _ret_kernel(lg_ref, q_ref, k_ref, v_ref, g_ref, gn_ref, *rest, seq_len, heads, has_state):
    if has_state:
        s0_ref, o_ref = rest
    else:
        o_ref, sfin_ref = rest
    hg = pl.program_id(1)
    for j in range(heads):
        hd = hg * heads + j
        cols = slice(j * RET_DK, (j + 1) * RET_DK)
        if has_state:
            s0_f, s0_b = s0_ref[0, j], s0_ref[1, j]
        else:
            s0_f = s0_b = jnp.zeros((RET_DK, RET_DV), F32)
        s_f, s_b = _ret_head(lg_ref[0, hd], lg_ref[1, hd], q_ref, k_ref, v_ref, g_ref, gn_ref, o_ref,
                             s0_f, s0_b, cols, seq_len)
        if not has_state:
            sfin_ref[0, j] = s_f
            sfin_ref[1, j] = s_b


def _retention(proj, lg, gn_g, s0, n_seq, seq_len, blk0, heads):
    H = RET_HEADS
    width = heads * RET_DK

    def col_spec(off):
        first = (S5_WIDTH + off * RET_WIDTH) // width
        return pl.BlockSpec((seq_len, width), lambda b, hg: (blk0 + b, first + hg))

    state_spec = pl.BlockSpec((None, 2, heads, RET_DK, RET_DV), lambda b, hg: (b, 0, hg, 0, 0))
    in_specs = [
        pl.BlockSpec(memory_space=pltpu.SMEM),
        col_spec(0), col_spec(1), col_spec(2), col_spec(3),
        pl.BlockSpec((1, width), lambda b, hg: (0, hg)),
    ]
    args = [lg, proj, proj, proj, proj, gn_g.reshape(1, RET_WIDTH)]
    out_specs = [pl.BlockSpec((seq_len, width), lambda b, hg: (b, hg))]
    out_shape = [jax.ShapeDtypeStruct((n_seq * seq_len, RET_WIDTH), BF16)]
    if s0 is not None:
        in_specs.append(state_spec)
        args.append(s0)
    else:
        out_specs.append(state_spec)
        out_shape.append(jax.ShapeDtypeStruct((n_seq, 2, H, RET_DK, RET_DV), F32))
    return pl.pallas_call(
        functools.partial(_ret_kernel, seq_len=seq_len, heads=heads, has_state=s0 is not None),
        grid=(n_seq, H // heads),
        in_specs=in_specs,
        out_specs=out_specs,
        out_shape=out_shape,
        compiler_params=_cparams(("parallel", "parallel")),
        name="retention_dec" if s0 is not None else "retention_ctx",
    )(*args)


def _shift_rows(x, d, ridx):
    n = x.shape[0]
    if d == 0:
        return x
    rolled = pltpu.roll(x, d % n, axis=0)
    valid = (ridx >= d) if d > 0 else (ridx < n + d)
    return jnp.where(valid, rolled, 0.0)


def _pool_kernel(u_ref, w_ref, sc_ref, o_ref, *, seq_len):
    L = seq_len
    ridx = lax.broadcasted_iota(jnp.int32, (L, POOL_GC), 0)
    for gi, w in enumerate(POOL_WINDOWS):
        cols = slice(gi * POOL_GC, (gi + 1) * POOL_GC)
        x = u_ref[:, cols]
        half = w // 2
        back, ahead = x, x
        span = 1
        while span < half:
            back = back + _shift_rows(back, span, ridx)
            ahead = ahead + _shift_rows(ahead, -span, ridx)
            span *= 2
        win = _shift_rows(back, 1, ridx) + ahead
        lo = jnp.maximum(ridx - half, 0)
        hi = jnp.minimum(ridx + half - 1, L - 1)
        mean = win / (hi - lo + 1).astype(F32)
        mixed = _dot((mean - x).astype(BF16), w_ref[gi].astype(BF16))
        o_ref[:, cols] = (mixed * sc_ref[:, cols]).astype(BF16)


def _pool(proj, w_pool, o, pool_scale, n_seq, seq_len, blk0):
    return pl.pallas_call(
        functools.partial(_pool_kernel, seq_len=seq_len),
        grid=(n_seq,),
        in_specs=[
            pl.BlockSpec((seq_len, POOL_WIDTH), lambda b: (blk0 + b, 0)),
            pl.BlockSpec((None, POOL_GROUPS, POOL_GC, POOL_GC), lambda b: (o, 0, 0, 0)),
            pl.BlockSpec((1, POOL_WIDTH), lambda b: (0, 0)),
        ],
        out_specs=pl.BlockSpec((seq_len, POOL_WIDTH), lambda b: (b, 0)),
        out_shape=jax.ShapeDtypeStruct((n_seq * seq_len, POOL_WIDTH), BF16),
        compiler_params=_cparams(("parallel",)),
        name="pool",
    )(proj, w_pool, pool_scale.reshape(1, POOL_WIDTH))


def _softmax_pv(s, v, sink):
    m = jnp.maximum(jnp.max(s, axis=-1, keepdims=True), sink)
    p = jnp.exp(s - m)
    denom = jnp.sum(p, axis=-1, keepdims=True) + jnp.exp(sink - m)
    return _dot(p.astype(BF16), v) / denom


def _qk(q, k):
    return lax.dot_general(q, k, (((1,), (1,)), ((), ())), preferred_element_type=F32) * (ATT_HD ** -0.5)


def _ctx_att_kernel(sink_ref, q_ref, k_ref, v_ref, o_ref, ko_ref, vo_ref):
    for kv in range(ATT_KV):
        kv_cols = slice(kv * ATT_HD, (kv + 1) * ATT_HD)
        ko_ref[:, kv, :] = k_ref[:, kv_cols]
        vo_ref[:, kv, :] = v_ref[:, kv_cols]
        k = k_ref[:, kv_cols].astype(BF16)
        v = v_ref[:, kv_cols].astype(BF16)
        for g in range(ATT_GROUP):
            head = kv * ATT_GROUP + g
            cols = slice(head * ATT_HD, (head + 1) * ATT_HD)
            s = _qk(q_ref[:, cols].astype(BF16), k)
            o_ref[:, cols] = _softmax_pv(s, v, sink_ref[head]).astype(BF16)


def _ctx_attention(proj, sink):
    qw = ATT_HEADS * ATT_HD
    kvw = ATT_KV * ATT_HD
    kb = (POOL_WIDTH + qw) // kvw
    return pl.pallas_call(
        _ctx_att_kernel,
        grid=(BATCH,),
        in_specs=[
            pl.BlockSpec(memory_space=pltpu.SMEM),
            pl.BlockSpec((SEQ, qw), lambda b: (b, POOL_WIDTH // qw)),
            pl.BlockSpec((SEQ, kvw), lambda b: (b, kb)),
            pl.BlockSpec((SEQ, kvw), lambda b: (b, kb + 1)),
        ],
        out_specs=[pl.BlockSpec((SEQ, qw), lambda b: (b, 0)),
                   pl.BlockSpec((None, SEQ, ATT_KV, ATT_HD), lambda b: (b, 0, 0, 0)),
                   pl.BlockSpec((None, SEQ, ATT_KV, ATT_HD), lambda b: (b, 0, 0, 0))],
        out_shape=[jax.ShapeDtypeStruct((N_CTX_TOK, qw), BF16),
                   jax.ShapeDtypeStruct((BATCH, SEQ, ATT_KV, ATT_HD), F32),
                   jax.ShapeDtypeStruct((BATCH, SEQ, ATT_KV, ATT_HD), F32)],
        compiler_params=_cparams(("parallel",)),
        name="attention_ctx",
    )(sink, proj, proj, proj)


def _rope(x, cos, sin):
    lane = lax.broadcasted_iota(jnp.int32, x.shape, 1)
    partner = jnp.where((lane & 63) < 32, pltpu.roll(x, 96, axis=1), pltpu.roll(x, 32, axis=1))
    return x * cos + partner * sin


def _dec_att_kernel(sink_ref, q_ref, k_ref, v_ref, kc_ref, vc_ref, cos_ref, sin_ref, o_ref, kr_ref):
    kv = pl.program_id(1)
    L, B = DEC_SEQ, ATT_BLOCK
    nb = L // B
    kr_ref[...] = _rope(k_ref[...], cos_ref[...], sin_ref[...]).astype(BF16)
    kc = kc_ref[...].astype(BF16)
    vc = vc_ref[...].astype(BF16)
    for i in range(nb):
        lo, hi = max(i - 1, 0) * B, min(i + 2, nb) * B
        rows = slice(i * B, (i + 1) * B)
        kwin = kr_ref[lo:hi, :]
        vals = jnp.concatenate([v_ref[lo:hi, :].astype(BF16), vc], axis=0)
        qpos = i * B + lax.broadcasted_iota(jnp.int32, (B, hi - lo), 0)
        kpos = lo + lax.broadcasted_iota(jnp.int32, (B, hi - lo), 1)
        band = jnp.abs(kpos - qpos) <= ATT_WIN
        for g in range(ATT_GROUP):
            cols = slice(g * ATT_HD, (g + 1) * ATT_HD)
            q = _rope(q_ref[rows, cols], cos_ref[rows, :], sin_ref[rows, :]).astype(BF16)
            s = jnp.concatenate([jnp.where(band, _qk(q, kwin), NEG), _qk(q, kc)], axis=1)
            o_ref[rows, cols] = _softmax_pv(s, vals, sink_ref[kv * ATT_GROUP + g]).astype(BF16)


def _rope_tables():
    half = ATT_HD // 2
    nfreq = half // 2
    freqs = ROPE_BASE ** (-jnp.arange(nfreq, dtype=F32) / nfreq)
    t = jnp.arange(DEC_SEQ)
    pos = jnp.stack([t // GRID_W, t % GRID_W], axis=1).astype(F32)
    ang = pos[:, :, None] * freqs[None, None, :]
    cos = jnp.repeat(jnp.cos(ang), 2, axis=1).reshape(DEC_SEQ, ATT_HD)
    sin = jnp.sin(ang)
    sin = jnp.stack([-sin, sin], axis=2).reshape(DEC_SEQ, ATT_HD)
    return cos, sin


def _dec_attention(proj, cache_k, cache_v, sink, cos, sin, o):
    qw = ATT_GROUP * ATT_HD
    qb = POOL_WIDTH // qw
    kb = (POOL_WIDTH + ATT_HEADS * ATT_HD) // ATT_HD
    blk0 = N_CTX_TOK // DEC_SEQ
    ck = cache_k.reshape(DEC_BATCH, N_ODD, PAST_LEN, ATT_KV * ATT_HD)
    cv = cache_v.reshape(DEC_BATCH, N_ODD, PAST_LEN, ATT_KV * ATT_HD)
    cache_spec = pl.BlockSpec((None, None, PAST_LEN, ATT_HD), lambda b, kv: (b, o, 0, kv))
    table_spec = pl.BlockSpec((DEC_SEQ, ATT_HD), lambda b, kv: (0, 0))
    return pl.pallas_call(
        _dec_att_kernel,
        grid=(DEC_BATCH, ATT_KV),
        in_specs=[
            pl.BlockSpec(memory_space=pltpu.SMEM),
            pl.BlockSpec((DEC_SEQ, qw), lambda b, kv: (blk0 + b, qb + kv)),
            pl.BlockSpec((DEC_SEQ, ATT_HD), lambda b, kv: (blk0 + b, kb + kv)),
            pl.BlockSpec((DEC_SEQ, ATT_HD), lambda b, kv: (blk0 + b, kb + ATT_KV + kv)),
            cache_spec, cache_spec, table_spec, table_spec,
        ],
        out_specs=pl.BlockSpec((DEC_SEQ, qw), lambda b, kv: (b, kv)),
        out_shape=jax.ShapeDtypeStruct((N_DEC_TOK, ATT_HEADS * ATT_HD), BF16),
        scratch_shapes=[pltpu.VMEM((DEC_SEQ, ATT_HD), BF16)],
        compiler_params=_cparams(("parallel", "parallel")),
        name="attention_dec",
    )(sink, proj, proj, proj, ck, cv, cos, sin)


def kernel(x_prompt, x_sample, c, state_s5_re, state_s5_im, state_ret, cache_k, cache_v, c_ctx, w_mod, b_mod, norm_g, ffn1_gate, ffn1_up, ffn1_down, ffn2_gate, ffn2_up, ffn2_down, even_w_in, even_w_out, s5_lam_re, s5_lam_im, s5_log_dt, s5_b_re, s5_b_im, s5_c_re, s5_c_im, s5_d, s5_glu_w, s5_glu_b, ret_decay_logit, ret_gn_g, odd_w_in, odd_w_out, pool_w, pool_scale, att_sink):
    x = jnp.concatenate([x_prompt.reshape(N_CTX_TOK, D_MODEL), x_sample.reshape(N_DEC_TOK, D_MODEL)], axis=0)
    c8 = jnp.concatenate([c_ctx[None, :], c, jnp.zeros((MOD_ROWS - 1 - DEC_BATCH, D_MODEL), F32)], axis=0)
    mod5 = _modulation(c8, w_mod, b_mod).reshape(DEPTH, MOD_ROWS, N_MOD, 1, D_MODEL)
    norm4 = norm_g.reshape(DEPTH, 6, 1, D_MODEL)
    cos, sin = _rope_tables()
    dec_blk = N_CTX_TOK // DEC_SEQ
    out_re, out_im, out_ret, out_k, out_v = [], [], [], [], []
    for l in range(DEPTH):
        x = _ffn(x, mod5, norm4, ffn1_gate, ffn1_up, ffn1_down, l, 0, 0, 1)
        if l % 2 == 0:
            e = l // 2
            proj = _proj_in(x, mod5, norm4, even_w_in, e, l)
            ops = _s5_operators(s5_lam_re[e], s5_lam_im[e], s5_log_dt[e], s5_b_re[e], s5_b_im[e],
                                s5_c_re[e], s5_c_im[e])
            h0 = jnp.stack([state_s5_re[:, e], state_s5_im[:, e]], axis=2)
            h0 = jnp.transpose(h0, (3, 1, 2, 4, 0)).reshape(S5_GROUPS, 4 * S5_P, DEC_BATCH)
            h0 = jnp.repeat(h0, S5_DEC_CHUNKS, axis=2)
            y_a, s_re, s_im = _s5_mixer(proj, ops, h0, s5_d[e], s5_glu_w, s5_glu_b[e], e)
            lg = jax.nn.log_sigmoid(ret_decay_logit[e].astype(F32))
            yb_ctx, s_ret = _retention(proj, lg, ret_gn_g[e], None, BATCH, SEQ, 0, RET_HEADS)
            (yb_dec,) = _retention(proj, lg, ret_gn_g[e], state_ret[:, e], DEC_BATCH, DEC_SEQ, dec_blk, 2)
            y_b = (yb_ctx, yb_dec)
            out_re.append(s_re)
            out_im.append(s_im)
            out_ret.append(s_ret)
            w_out, w_idx = even_w_out, e
        else:
            o = l // 2
            proj = _proj_in(x, mod5, norm4, odd_w_in, o, l)
            y_a = (_pool(proj, pool_w, o, pool_scale[o], BATCH, SEQ, 0),
                   _pool(proj, pool_w, o, pool_scale[o], DEC_BATCH, DEC_SEQ, dec_blk))
            sink = att_sink[o].astype(F32)
            yd_ctx, k_ctx, v_ctx = _ctx_attention(proj, sink)
            y_b = (yd_ctx, _dec_attention(proj, cache_k, cache_v, sink, cos, sin, o))
            out_k.append(k_ctx)
            out_v.append(v_ctx)
            w_out, w_idx = odd_w_out, o
        x = _proj_out(x, y_a, y_b, mod5, norm4, w_out, w_idx, l)
        if l < DEPTH - 1:
            x = _ffn(x, mod5, norm4, ffn2_gate, ffn2_up, ffn2_down, l, 2, 4, 5)
    ffn2 = functools.partial(_ffn, x, mod5, norm4, ffn2_gate, ffn2_up, ffn2_down, DEPTH - 1, 2, 4, 5)
    y_prompt = ffn2(rows=(0, N_CTX_TOK)).reshape(BATCH, SEQ, D_MODEL)
    y_sample = ffn2(rows=(N_CTX_TOK, N_TOK)).reshape(DEC_BATCH, DEC_SEQ, D_MODEL)
    return (y_prompt, y_sample, jnp.stack(out_re, axis=1), jnp.stack(out_im, axis=1),
            jnp.stack(out_ret, axis=1), jnp.stack(out_k, axis=1), jnp.stack(out_v, axis=1))
```

```python
import functools
import math

import jax
import jax.numpy as jnp
from jax import lax
from jax.experimental import pallas as pl
from jax.experimental.pallas import tpu as pltpu

F32 = jnp.float32
BF16 = jnp.bfloat16

D_MODEL = 2048
BATCH = 16
SEQ = 256
DEPTH = 4
DEC_BATCH = 2
DEC_SEQ = 1024
PAST_LEN = 256
GRID_W = 64
N_EVEN = (DEPTH + 1) // 2
N_ODD = DEPTH // 2
N_MOD = 9
FFN_HIDDEN = 5632
EPS = 1e-6
NEG = -1e30

S5_WIDTH = D_MODEL // 2
S5_GROUP = 16
S5_GROUPS = S5_WIDTH // S5_GROUP
S5_P = 64
RET_HEADS = 8
RET_DK = (D_MODEL // 2) // RET_HEADS
RET_DV = RET_DK
RET_WIDTH = RET_HEADS * RET_DV
RET_CHUNK = 128
EVEN_IN = S5_WIDTH + 4 * RET_WIDTH

POOL_WIDTH = D_MODEL // 2
POOL_WINDOWS = (2, 4, 8, 16)
POOL_GROUPS = 4
POOL_GC = POOL_WIDTH // POOL_GROUPS
ATT_HEADS = 8
ATT_KV = 2
ATT_HD = (D_MODEL // 2) // ATT_HEADS
ATT_GROUP = ATT_HEADS // ATT_KV
ATT_WIN = 128
ATT_BLOCK = 128
ODD_IN = POOL_WIDTH + (ATT_HEADS + 2 * ATT_KV) * ATT_HD
ROPE_BASE = 10000.0

N_CTX_TOK = BATCH * SEQ
N_DEC_TOK = DEC_BATCH * DEC_SEQ
N_TOK = N_CTX_TOK + N_DEC_TOK
MOD_ROWS = 8

S5_T = 16
S5_LANES = S5_T * S5_GROUP
S5_CTX_CHUNKS = SEQ // S5_T
S5_DEC_CHUNKS = DEC_SEQ // S5_T
S5_CTX_ROWS = BATCH * S5_CTX_CHUNKS
S5_DEC_ROWS = DEC_BATCH * S5_DEC_CHUNKS
S5_ROWS = S5_CTX_ROWS + S5_DEC_ROWS
S5_NPOW = 6

VMEM_LIMIT = 60 * 1024 * 1024


def _cparams(sem):
    return pltpu.CompilerParams(dimension_semantics=sem, vmem_limit_bytes=VMEM_LIMIT)


def _dot(a, b):
    return jnp.dot(a, b, preferred_element_type=F32)


def _rms(x, g):
    return (x * lax.rsqrt(jnp.mean(x * x, axis=-1, keepdims=True) + EPS)) * g


def _mod_row(i, tm):
    start = i * tm
    return (start >= N_CTX_TOK).astype(jnp.int32) + (start >= N_CTX_TOK + DEC_SEQ).astype(jnp.int32)


def _mod_kernel(c_ref, w_ref, b_ref, o_ref):
    c = c_ref[...]
    s = (c * jax.nn.sigmoid(c)).astype(BF16)
    o_ref[:, 0, :] = _dot(s, w_ref[...].astype(BF16)) + b_ref[...]


def _modulation(c8, w_mod, b_mod):
    n = N_MOD * D_MODEL
    return pl.pallas_call(
        _mod_kernel,
        grid=(DEPTH, N_MOD),
        in_specs=[
            pl.BlockSpec((MOD_ROWS, D_MODEL), lambda l, j: (0, 0)),
            pl.BlockSpec((None, D_MODEL, D_MODEL), lambda l, j: (l, 0, j)),
            pl.BlockSpec((None, 1, D_MODEL), lambda l, j: (l, 0, j)),
        ],
        out_specs=pl.BlockSpec((None, MOD_ROWS, None, 1, D_MODEL), lambda l, j: (l, 0, j, 0, 0)),
        out_shape=jax.ShapeDtypeStruct((DEPTH, MOD_ROWS, N_MOD, 1, D_MODEL), F32),
        compiler_params=_cparams(("arbitrary", "arbitrary")),
        name="modulation",
    )(c8, w_mod, b_mod.reshape(DEPTH, 1, n))


def _mod_spec(l, j, tm, tile0=0):
    return pl.BlockSpec((None, None, None, 1, D_MODEL),
                        lambda i, *_: (l, _mod_row(tile0 + i, tm), j, 0, 0))


def _gain_spec(l, j):
    return pl.BlockSpec((None, None, 1, D_MODEL), lambda i, *_: (l, j, 0, 0))


FFN_TM = 1024
FFN_TF = 256
SLAB_ROWS = 256


def _adaln_in(x, g_ref, sc_ref, sh_ref):
    return (_rms(x, g_ref[...]) * (1.0 + sc_ref[...]) + sh_ref[...]).astype(BF16)


def _ffn_kernel(x_ref, sh_ref, sc_ref, gate_ref, gin_ref, gout_ref, wg_ref, wu_ref, wd_ref,
                o_ref, h_ref, *, nf, tm):
    f = pl.program_id(1)
    slabs = [slice(r0, r0 + SLAB_ROWS) for r0 in range(0, tm, SLAB_ROWS)]

    def weights():
        return wg_ref[...].astype(BF16), wu_ref[...].astype(BF16), wd_ref[...].astype(BF16)

    def swiglu(h, w):
        g = _dot(h, w[0])
        u = _dot(h, w[1])
        a = ((g * jax.nn.sigmoid(g)) * u).astype(BF16)
        return _dot(a, w[2])

    @pl.when(f == 0)
    def _():
        w = weights()
        for rows in slabs:
            h = _adaln_in(x_ref[rows, :], gin_ref, sc_ref, sh_ref)
            h_ref[rows, :] = h
            o_ref[rows, :] = swiglu(h, w)

    @pl.when(jnp.logical_and(f > 0, f < nf - 1))
    def _():
        o_ref[...] += swiglu(h_ref[...], weights())

    @pl.when(f == nf - 1)
    def _():
        w = weights()
        for rows in slabs:
            y = o_ref[rows, :] + swiglu(h_ref[rows, :], w)
            o_ref[rows, :] = x_ref[rows, :] + (0.5 * gate_ref[...]) * _rms(y, gout_ref[...])


def _ffn(x, mod5, norm4, w_gate, w_up, w_down, l, j, g_in, g_out, rows=(0, N_TOK)):
    tm, tf = FFN_TM, FFN_TF
    nf = FFN_HIDDEN // tf
    tile0, n_tiles = rows[0] // tm, (rows[1] - rows[0]) // tm
    return pl.pallas_call(
        functools.partial(_ffn_kernel, nf=nf, tm=tm),
        grid=(n_tiles, nf),
        in_specs=[
            pl.BlockSpec((tm, D_MODEL), lambda i, f: (tile0 + i, 0)),
            _mod_spec(l, 3 * j, tm, tile0), _mod_spec(l, 3 * j + 1, tm, tile0), _mod_spec(l, 3 * j + 2, tm, tile0),
            _gain_spec(l, g_in), _gain_spec(l, g_out),
            pl.BlockSpec((None, D_MODEL, tf), lambda i, f: (l, 0, f)),
            pl.BlockSpec((None, D_MODEL, tf), lambda i, f: (l, 0, f)),
            pl.BlockSpec((None, tf, D_MODEL), lambda i, f: (l, f, 0)),
        ],
        out_specs=pl.BlockSpec((tm, D_MODEL), lambda i, f: (i, 0)),
        out_shape=jax.ShapeDtypeStruct((n_tiles * tm, D_MODEL), F32),
        scratch_shapes=[pltpu.VMEM((tm, D_MODEL), BF16)],
        compiler_params=_cparams(("parallel", "arbitrary")),
        name="ffn",
    )(x, mod5, mod5, mod5, norm4, norm4, w_gate, w_up, w_down)


PROJ_TM = 1024
PROJ_TN = 1280


def _proj_in_kernel(x_ref, sh_ref, sc_ref, gin_ref, w_ref, o_ref, h_ref, *, tm):
    k = pl.program_id(1)

    @pl.when(k == 0)
    def _():
        w = w_ref[...].astype(BF16)
        for r0 in range(0, tm, SLAB_ROWS):
            rows = slice(r0, r0 + SLAB_ROWS)
            h = _adaln_in(x_ref[rows, :], gin_ref, sc_ref, sh_ref)
            h_ref[rows, :] = h
            o_ref[rows, :] = _dot(h, w)

    @pl.when(k > 0)
    def _():
        o_ref[...] = _dot(h_ref[...], w_ref[...].astype(BF16))


def _proj_in(x, mod5, norm4, w_in, e, l):
    tm, tn = PROJ_TM, PROJ_TN
    n = w_in.shape[-1]
    return pl.pallas_call(
        functools.partial(_proj_in_kernel, tm=tm),
        grid=(N_TOK // tm, n // tn),
        in_specs=[
            pl.BlockSpec((tm, D_MODEL), lambda i, k: (i, 0)),
            _mod_spec(l, 3, tm), _mod_spec(l, 4, tm),
            _gain_spec(l, 2),
            pl.BlockSpec((None, D_MODEL, tn), lambda i, k: (e, 0, k)),
        ],
        out_specs=pl.BlockSpec((tm, tn), lambda i, k: (i, k)),
        out_shape=jax.ShapeDtypeStruct((N_TOK, n), F32),
        scratch_shapes=[pltpu.VMEM((tm, D_MODEL), BF16)],
        compiler_params=_cparams(("parallel", "arbitrary")),
        name="proj_in",
    )(x, mod5, mod5, norm4, w_in)


OUT_TM = 512


OUT_ROWS = 256


def _proj_out_kernel(x_ref, yac_ref, yad_ref, ybc_ref, ybd_ref, gate_ref, gout_ref, w_ref, o_ref, *, tm):
    def run(ya_ref, yb_ref):
        w = w_ref[...].astype(BF16)
        for r0 in range(0, tm, OUT_ROWS):
            rows = slice(r0, r0 + OUT_ROWS)
            y = _dot(jnp.concatenate([ya_ref[rows, :], yb_ref[rows, :]], axis=1), w)
            o_ref[rows, :] = x_ref[rows, :] + gate_ref[...] * _rms(y, gout_ref[...])

    is_ctx = pl.program_id(0) < N_CTX_TOK // tm
    pl.when(is_ctx)(lambda: run(yac_ref, ybc_ref))
    pl.when(jnp.logical_not(is_ctx))(lambda: run(yad_ref, ybd_ref))


def _proj_out(x, ya, yb, mod5, norm4, w_out, e, l):
    tm = OUT_TM
    half = D_MODEL // 2
    nct = N_CTX_TOK // tm

    def split_specs(y):
        ctx_spec = pl.BlockSpec((tm, half), lambda i: (jnp.minimum(i, nct - 1), 0))
        if isinstance(y, tuple):
            return y, (ctx_spec, pl.BlockSpec((tm, half), lambda i: (jnp.maximum(i - nct, 0), 0)))
        return (y, y), (ctx_spec, pl.BlockSpec((tm, half), lambda i: (jnp.maximum(i, nct), 0)))

    (ya_c, ya_d), (sa_c, sa_d) = split_specs(ya)
    (yb_c, yb_d), (sb_c, sb_d) = split_specs(yb)
    return pl.pallas_call(
        functools.partial(_proj_out_kernel, tm=tm),
        grid=(N_TOK // tm,),
        in_specs=[
            pl.BlockSpec((tm, D_MODEL), lambda i: (i, 0)),
            sa_c, sa_d, sb_c, sb_d,
            _mod_spec(l, 5, tm),
            _gain_spec(l, 3),
            pl.BlockSpec((None, D_MODEL, D_MODEL), lambda i: (e, 0, 0), pipeline_mode=pl.Buffered(1)),
        ],
        out_specs=pl.BlockSpec((tm, D_MODEL), lambda i: (i, 0)),
        out_shape=jax.ShapeDtypeStruct((N_TOK, D_MODEL), F32),
        compiler_params=_cparams(("parallel",)),
        name="proj_out",
    )(x, ya_c, ya_d, yb_c, yb_d, mod5, norm4, w_out)


def _s5_operators(lam_re, lam_im, log_dt, b_re, b_im, c_re, c_im):
    T, P, G = S5_T, S5_P, S5_GROUPS
    lam = lax.complex(jnp.minimum(lam_re.astype(F32), -1e-4), lam_im.astype(F32))
    a = jnp.exp(lam * jnp.exp(log_dt.astype(F32))[..., None])
    b_t = jnp.swapaxes(lax.complex(b_re.astype(F32), b_im.astype(F32)), 2, 3)
    bbar = ((a - 1.0) / lam)[:, :, None, :] * b_t
    c = lax.complex(c_re.astype(F32), c_im.astype(F32))
    pows = [jnp.ones_like(a), a]
    for _ in range(T - 1):
        pows.append(pows[-1] * a)
    pw = jnp.stack(pows)
    inv = 1.0 / pw
    up, down = pw[1:], pw[1:][::-1]
    zf = jnp.concatenate([pw[:T, 0][::-1], inv[1:, 0], up[:, 0], pw[T:, 0]])
    zb = jnp.concatenate([pw[:T, 1], inv[1:, 1][::-1], down[:, 1], pw[T:, 1]])

    def parts(zf, zb):
        z = jnp.stack([jnp.real(zf), jnp.imag(zf), jnp.real(zb), jnp.imag(zb)])
        return z.reshape(4, z.shape[1], G * P)

    def rows(z):
        return parts(jnp.swapaxes(z[0], 0, 1), jnp.swapaxes(z[1], 0, 1))

    gap = jnp.zeros((4, S5_F_B - (3 * T + 1), G * P), F32)
    tail = jnp.zeros((4, 128 - S5_F_C - S5_GROUP, G * P), F32)
    return jnp.concatenate([parts(zf, zb), gap, rows(bbar), rows(c), tail], axis=1)


S5_F_INC, S5_F_KIN, S5_F_OUT, S5_F_A, S5_F_B, S5_F_C = 0, S5_T, 2 * S5_T, 3 * S5_T, 4 * S5_T, 5 * S5_T


def _cmul_rows(x, ar, ai):
    h = x.shape[0] // 2
    xr, xi = x[:h], x[h:]
    return jnp.concatenate([xr * ar - xi * ai, xr * ai + xi * ar], axis=0)


def _s5_lane_scan(inc, a_rows, h0, lc):
    P = S5_P
    cidx = lax.broadcasted_iota(jnp.int32, (2 * P, 128), 1) & (lc - 1)
    fwd, bwd = inc[:2 * P], inc[2 * P:]
    arf, aif, arb, aib = a_rows[:P], a_rows[P:2 * P], a_rows[2 * P:3 * P], a_rows[3 * P:]
    if h0 is not None:
        fwd = fwd + jnp.where(cidx == 0, _cmul_rows(h0[:2 * P], arf, aif), 0.0)
        bwd = bwd + jnp.where(cidx == lc - 1, _cmul_rows(h0[2 * P:], arb, aib), 0.0)
    d = 1
    while d < lc:
        sh = jnp.where(cidx >= d, pltpu.roll(fwd, d, axis=1), 0.0)
        fwd = fwd + _cmul_rows(sh, arf, aif)
        sh = jnp.where(cidx < lc - d, pltpu.roll(bwd, 128 - d, axis=1), 0.0)
        bwd = bwd + _cmul_rows(sh, arb, aib)
        d *= 2
        if d < lc:
            arf, aif = arf * arf - aif * aif, 2.0 * arf * aif
            arb, aib = arb * arb - aib * aib, 2.0 * arb * aib
    first_f = 0.0 if h0 is None else h0[:2 * P]
    first_b = 0.0 if h0 is None else h0[2 * P:]
    start_f = jnp.where(cidx >= 1, pltpu.roll(fwd, 1, axis=1), first_f)
    start_b = jnp.where(cidx < lc - 1, pltpu.roll(bwd, 127, axis=1), first_b)
    return jnp.concatenate([start_f, start_b], axis=0), jnp.concatenate([fwd, bwd], axis=0)


def _dot_split(a, b):
    a_hi = a.astype(BF16)
    b_hi = b.astype(BF16)
    a_lo = (a - a_hi.astype(F32)).astype(BF16)
    b_lo = (b - b_hi.astype(F32)).astype(BF16)
    return _dot(a_hi, b_hi) + (_dot(a_hi, b_lo) + _dot(a_lo, b_hi))


S5_SLAB = 128 // S5_GROUP


def _split3(x):
    hi = x.astype(BF16)
    r = x - hi.astype(F32)
    mid = r.astype(BF16)
    return hi, mid, (r - mid.astype(F32)).astype(BF16)


def _cprod_rows(a, b, conj_sign):
    h = a.shape[0] // 2
    ar, ai, br, bi = a[:h], a[h:], b[:h], b[h:]
    return jnp.concatenate([ar * br - ai * bi, conj_sign * (ar * bi + ai * br)], axis=0)


def _s5_kernel(u_ref, f_ref, h0_ref, y_ref, fin_ref, xt_ref, yt_ref, fs_ref):
    T, K, P, R = S5_T, S5_GROUP, S5_P, S5_ROWS
    F = S5_LANES
    for gg in range(S5_SLAB):
        fs_ref[gg] = jnp.concatenate([f_ref[part, :, gg * P:(gg + 1) * P] for part in range(4)], axis=1)
    for s in range(T):
        ut = u_ref[pl.ds(s, R, stride=T), :].T
        for gg in range(S5_SLAB):
            xt_ref[gg, s * K:(s + 1) * K, :] = ut[gg * K:(gg + 1) * K, :].astype(BF16)
    shift = K.bit_length() - 1
    tok_out = lax.broadcasted_iota(jnp.int32, (F, F), 0) >> shift
    tok_in = lax.broadcasted_iota(jnp.int32, (F, F), 1) >> shift
    lane = lax.broadcasted_iota(jnp.int32, (128, F), 0)
    feat = lax.broadcasted_iota(jnp.int32, (128, F), 1)
    tok, ch = feat >> shift, feat & (K - 1)

    def spread(field, index):
        return jnp.where(lane == field + index, 1.0, 0.0).astype(BF16)

    e_inc, e_kin, e_out = spread(S5_F_INC, tok), spread(S5_F_KIN, tok), spread(S5_F_OUT, tok)
    e_b, e_c = spread(S5_F_B, ch), spread(S5_F_C, ch)
    lane_a = lax.broadcasted_iota(jnp.int32, (128, 128), 0)
    e_a = jnp.where(lane_a == S5_F_A, 1.0, 0.0).astype(BF16)
    seq = lax.broadcasted_iota(jnp.int32, (128, 128), 1)
    nseq = 128 // S5_CTX_CHUNKS
    sel_last = [jnp.where((lane_a == (seq - o) * S5_CTX_CHUNKS + S5_CTX_CHUNKS - 1) & (seq >= o) & (seq < o + nseq),
                          1.0, 0.0).astype(BF16) for o in (0, nseq)]
    sel_first = [jnp.where((lane_a == (seq - o) * S5_CTX_CHUNKS) & (seq >= o) & (seq < o + nseq),
                           1.0, 0.0).astype(BF16) for o in (0, nseq)]

    def group(gg, carry):
        hi, mid, lo = _split3(fs_ref[gg].T)

        def expand(e, exact=False):
            out = _dot(hi, e) + _dot(mid, e)
            return out + _dot(lo, e) if exact else out

        def cprod(pw, other, conj_sign):
            return jnp.concatenate([_cprod_rows(pw[:2 * P], other[:2 * P], conj_sign),
                                    _cprod_rows(pw[2 * P:], other[2 * P:], conj_sign)], axis=0)

        bb = expand(e_b)
        inc_op = cprod(expand(e_inc), bb, 1.0)
        kin_op = cprod(expand(e_kin), bb, 1.0)
        out_op = cprod(expand(e_out), expand(e_c), -1.0).T
        a_rows = expand(e_a, exact=True)
        resp_f = _dot_split(out_op[:, :2 * P], kin_op[:2 * P])
        resp_b = _dot_split(out_op[:, 2 * P:], kin_op[2 * P:])
        resp = jnp.where(tok_out >= tok_in, resp_f, 0.0) + jnp.where(tok_out <= tok_in, resp_b, 0.0)
        ops = jnp.concatenate([resp, inc_op], axis=0).astype(BF16)
        z = _dot(ops, xt_ref[gg])
        lanes = S5_CTX_ROWS // 2
        st_a, sc_a = _s5_lane_scan(z[F:, :lanes], a_rows, None, S5_CTX_CHUNKS)
        st_b, sc_b = _s5_lane_scan(z[F:, lanes:2 * lanes], a_rows, None, S5_CTX_CHUNKS)
        st_d, _ = _s5_lane_scan(z[F:, 2 * lanes:], a_rows, h0_ref[gg], S5_DEC_CHUNKS)
        start = jnp.concatenate([st_a, st_b, st_d], axis=1).astype(BF16)
        yt_ref[gg] = z[:F] + _dot(out_op.astype(BF16), start)
        fin_f = sum(_dot(t, sel_last[0]) for t in _split3(sc_a[:2 * P])) \
            + sum(_dot(t, sel_last[1]) for t in _split3(sc_b[:2 * P]))
        fin_b = sum(_dot(t, sel_first[0]) for t in _split3(sc_a[2 * P:])) \
            + sum(_dot(t, sel_first[1]) for t in _split3(sc_b[2 * P:]))
        fin_ref[gg] = jnp.concatenate([fin_f, fin_b], axis=0)
        return carry

    lax.fori_loop(0, S5_SLAB, group, 0)
    for i in range(T):
        yt = jnp.concatenate([yt_ref[gg, i * K:(i + 1) * K, :] for gg in range(S5_SLAB)], axis=0)
        y_ref[pl.ds(i, R, stride=T), :] = yt.T


def _s5_scan(proj, factors, h0):
    G, F, P4 = S5_GROUPS, S5_LANES, 4 * S5_P
    assert S5_CTX_ROWS == 256 and S5_DEC_ROWS == 128 and F == 256 and P4 == 256 and S5_T == S5_GROUP
    op_spec = pl.BlockSpec((S5_SLAB, P4, 128), lambda j: (j, 0, 0))
    factor_spec = pl.BlockSpec((4, 128, S5_SLAB * S5_P), lambda j: (0, 0, j))
    return pl.pallas_call(
        _s5_kernel,
        grid=(G // S5_SLAB,),
        in_specs=[pl.BlockSpec((N_TOK, 128), lambda j: (0, j)), factor_spec, op_spec],
        out_specs=[pl.BlockSpec((N_TOK, 128), lambda j: (0, j)), op_spec],
        out_shape=[
            jax.ShapeDtypeStruct((N_TOK, S5_WIDTH), F32),
            jax.ShapeDtypeStruct((G, P4, 128), F32),
        ],
        scratch_shapes=[pltpu.VMEM((S5_SLAB, F, S5_ROWS), BF16), pltpu.VMEM((S5_SLAB, F, S5_ROWS), F32),
                        pltpu.VMEM((S5_SLAB, 128, P4), F32)],
        compiler_params=_cparams(("parallel",)),
        name="s5_scan",
    )(proj, factors, h0)


S5_POST_TM = 512


def _s5_post_kernel(y_ref, u_ref, d_ref, w_ref, b_ref, o_ref):
    y = y_ref[...] + d_ref[...] * u_ref[...]
    z = 0.5 * y * (1.0 + jnp.tanh(math.sqrt(2.0 / math.pi) * (y + 0.044715 * (y * y * y))))
    gate = jax.nn.sigmoid(_dot(z.astype(BF16), w_ref[...].astype(BF16)) + b_ref[...])
    o_ref[...] = (z * gate).astype(BF16)


def _s5_post(y, proj, d_skip, w_glu, b_glu, e):
    tm = S5_POST_TM
    return pl.pallas_call(
        _s5_post_kernel,
        grid=(N_TOK // tm,),
        in_specs=[
            pl.BlockSpec((tm, S5_WIDTH), lambda i: (i, 0)),
            pl.BlockSpec((tm, S5_WIDTH), lambda i: (i, 0)),
            pl.BlockSpec((1, S5_WIDTH), lambda i: (0, 0)),
            pl.BlockSpec((None, S5_WIDTH, S5_WIDTH), lambda i: (e, 0, 0), pipeline_mode=pl.Buffered(1)),
            pl.BlockSpec((1, S5_WIDTH), lambda i: (0, 0)),
        ],
        out_specs=pl.BlockSpec((tm, S5_WIDTH), lambda i: (i, 0)),
        out_shape=jax.ShapeDtypeStruct((N_TOK, S5_WIDTH), BF16),
        compiler_params=_cparams(("parallel",)),
        name="s5_post",
    )(y, proj, d_skip.reshape(1, S5_WIDTH), w_glu, b_glu.reshape(1, S5_WIDTH))


def _s5_mixer(proj, ops, h0, d_skip, w_glu, b_glu, e):
    y, fin = _s5_scan(proj, ops, h0)
    y_a = _s5_post(y, proj, d_skip, w_glu, b_glu, e)
    fin = fin[:, :, :BATCH].reshape(S5_GROUPS, 2, 2, S5_P, BATCH)
    fin = jnp.transpose(fin, (4, 1, 2, 0, 3))
    return y_a, fin[:, :, 0], fin[:, :, 1]


def _ret_head(lgf, lgb, q_ref, k_ref, v_ref, g_ref, gn_ref, o_ref, s0_f, s0_b, cols, seq_len):
    C = RET_CHUNK
    nc = seq_len // C
    row = lax.broadcasted_iota(jnp.int32, (C, C), 0).astype(F32)
    col = lax.broadcasted_iota(jnp.int32, (C, C), 1).astype(F32)
    rel = row - col
    decay = (jnp.where(rel >= 0, jnp.exp(jnp.maximum(rel, 0.0) * lgf), 0.0)
             + jnp.where(rel <= 0, jnp.exp(jnp.maximum(-rel, 0.0) * lgb), 0.0))
    kdec_f = jnp.exp((C - 1.0 - row) * lgf)
    kdec_b = jnp.exp(row * lgb)
    qdec_f = jnp.exp((row + 1.0) * lgf)
    qdec_b = jnp.exp((C - row) * lgb)
    cd_f = jnp.exp(jnp.full((1, RET_DV), C * lgf, F32))
    cd_b = jnp.exp(jnp.full((1, RET_DV), C * lgb, F32))

    def chunk(ref, c):
        return ref[c * C:(c + 1) * C, cols]

    tdot = functools.partial(lax.dot_general, dimension_numbers=(((0,), (0,)), ((), ())),
                             preferred_element_type=F32)
    outs = []
    kv_f, kv_b = [], []
    for c in range(nc):
        q = chunk(q_ref, c)
        k = chunk(k_ref, c) * (RET_DK ** -0.5)
        v = chunk(v_ref, c).astype(BF16)
        inner = lax.dot_general(q.astype(BF16), k.astype(BF16), (((1,), (1,)), ((), ())),
                                preferred_element_type=F32) * decay
        outs.append(_dot(inner.astype(BF16), v))
        kv_f.append(tdot((k * kdec_f).astype(BF16), v))
        kv_b.append(tdot((k * kdec_b).astype(BF16), v))
    s_f = s0_f
    for c in range(nc):
        outs[c] = outs[c] + _dot((chunk(q_ref, c) * qdec_f).astype(BF16), s_f.astype(BF16))
        s_f = cd_f * s_f + kv_f[c]
    s_b = s0_b
    for c in reversed(range(nc)):
        outs[c] = outs[c] + _dot((chunk(q_ref, c) * qdec_b).astype(BF16), s_b.astype(BF16))
        s_b = cd_b * s_b + kv_b[c]
    for c in range(nc):
        o = outs[c]
        mu = jnp.mean(o, axis=-1, keepdims=True)
        var = jnp.mean(jnp.square(o - mu), axis=-1, keepdims=True)
        o = ((o - mu) * lax.rsqrt(var + EPS)) * gn_ref[:, cols]
        g = chunk(g_ref, c)
        o_ref[c * C:(c + 1) * C, cols] = ((g * jax.nn.sigmoid(g)) * o).astype(BF16)
    return s_f, s_b


def _ret_kernel(lg_ref, q_ref, k_ref, v_ref, g_ref, gn_ref, *rest, seq_len, heads, has_state):
    if has_state:
        s0_ref, o_ref = rest
    else:
        o_ref, sfin_ref = rest
    hg = pl.program_id(1)
    for j in range(heads):
        hd = hg * heads + j
        cols = slice(j * RET_DK, (j + 1) * RET_DK)
        if has_state:
            s0_f, s0_b = s0_ref[0, j], s0_ref[1, j]
        else:
            s0_f = s0_b = jnp.zeros((RET_DK, RET_DV), F32)
        s_f, s_b = _ret_head(lg_ref[0, hd], lg_ref[1, hd], q_ref, k_ref, v_ref, g_ref, gn_ref, o_ref,
                             s0_f, s0_b, cols, seq_len)
        if not has_state:
            sfin_ref[0, j] = s_f
            sfin_ref[1, j] = s_b


def _retention(proj, lg, gn_g, s0, n_seq, seq_len, blk0, heads):
    H = RET_HEADS
    width = heads * RET_DK

    def col_spec(off):
        first = (S5_WIDTH + off * RET_WIDTH) // width
        return pl.BlockSpec((seq_len, width), lambda b, hg: (blk0 + b, first + hg))

    state_spec = pl.BlockSpec((None, 2, heads, RET_DK, RET_DV), lambda b, hg: (b, 0, hg, 0, 0))
    in_specs = [
        pl.BlockSpec(memory_space=pltpu.SMEM),
        col_spec(0), col_spec(1), col_spec(2), col_spec(3),
        pl.BlockSpec((1, width), lambda b, hg: (0, hg)),
    ]
    args = [lg, proj, proj, proj, proj, gn_g.reshape(1, RET_WIDTH)]
    out_specs = [pl.BlockSpec((seq_len, width), lambda b, hg: (b, hg))]
    out_shape = [jax.ShapeDtypeStruct((n_seq * seq_len, RET_WIDTH), BF16)]
    if s0 is not None:
        in_specs.append(state_spec)
        args.append(s0)
    else:
        out_specs.append(state_spec)
        out_shape.append(jax.ShapeDtypeStruct((n_seq, 2, H, RET_DK, RET_DV), F32))
    return pl.pallas_call(
        functools.partial(_ret_kernel, seq_len=seq_len, heads=heads, has_state=s0 is not None),
        grid=(n_seq, H // heads),
        in_specs=in_specs,
        out_specs=out_specs,
        out_shape=out_shape,
        compiler_params=_cparams(("parallel", "parallel")),
        name="retention_dec" if s0 is not None else "retention_ctx",
    )(*args)


def _shift_rows(x, d, ridx):
    n = x.shape[0]
    if d == 0:
        return x
    rolled = pltpu.roll(x, d % n, axis=0)
    valid = (ridx >= d) if d > 0 else (ridx < n + d)
    return jnp.where(valid, rolled, 0.0)


def _pool_kernel(u_ref, w_ref, sc_ref, o_ref, *, seq_len):
    L = seq_len
    ridx = lax.broadcasted_iota(jnp.int32, (L, POOL_GC), 0)
    for gi, w in enumerate(POOL_WINDOWS):
        cols = slice(gi * POOL_GC, (gi + 1) * POOL_GC)
        x = u_ref[:, cols]
        half = w // 2
        back, ahead = x, x
        span = 1
        while span < half:
            back = back + _shift_rows(back, span, ridx)
            ahead = ahead + _shift_rows(ahead, -span, ridx)
            span *= 2
        win = _shift_rows(back, 1, ridx) + ahead
        lo = jnp.maximum(ridx - half, 0)
        hi = jnp.minimum(ridx + half - 1, L - 1)
        mean = win / (hi - lo + 1).astype(F32)
        mixed = _dot((mean - x).astype(BF16), w_ref[gi].astype(BF16))
        o_ref[:, cols] = (mixed * sc_ref[:, cols]).astype(BF16)


def _pool(proj, w_pool, o, pool_scale, n_seq, seq_len, blk0):
    return pl.pallas_call(
        functools.partial(_pool_kernel, seq_len=seq_len),
        grid=(n_seq,),
        in_specs=[
            pl.BlockSpec((seq_len, POOL_WIDTH), lambda b: (blk0 + b, 0)),
            pl.BlockSpec((None, POOL_GROUPS, POOL_GC, POOL_GC), lambda b: (o, 0, 0, 0)),
            pl.BlockSpec((1, POOL_WIDTH), lambda b: (0, 0)),
        ],
        out_specs=pl.BlockSpec((seq_len, POOL_WIDTH), lambda b: (b, 0)),
        out_shape=jax.ShapeDtypeStruct((n_seq * seq_len, POOL_WIDTH), BF16),
        compiler_params=_cparams(("parallel",)),
        name="pool",
    )(proj, w_pool, pool_scale.reshape(1, POOL_WIDTH))


def _softmax_pv(s, v, sink):
    m = jnp.maximum(jnp.max(s, axis=-1, keepdims=True), sink)
    p = jnp.exp(s - m)
    denom = jnp.sum(p, axis=-1, keepdims=True) + jnp.exp(sink - m)
    return _dot(p.astype(BF16), v) / denom


def _qk(q, k):
    return lax.dot_general(q, k, (((1,), (1,)), ((), ())), preferred_element_type=F32) * (ATT_HD ** -0.5)


def _ctx_att_kernel(sink_ref, q_ref, k_ref, v_ref, o_ref, ko_ref, vo_ref):
    for kv in range(ATT_KV):
        kv_cols = slice(kv * ATT_HD, (kv + 1) * ATT_HD)
        ko_ref[:, kv, :] = k_ref[:, kv_cols]
        vo_ref[:, kv, :] = v_ref[:, kv_cols]
        k = k_ref[:, kv_cols].astype(BF16)
        v = v_ref[:, kv_cols].astype(BF16)
        for g in range(ATT_GROUP):
            head = kv * ATT_GROUP + g
            cols = slice(head * ATT_HD, (head + 1) * ATT_HD)
            s = _qk(q_ref[:, cols].astype(BF16), k)
            o_ref[:, cols] = _softmax_pv(s, v, sink_ref[head]).astype(BF16)


def _ctx_attention(proj, sink):
    qw = ATT_HEADS * ATT_HD
    kvw = ATT_KV * ATT_HD
    kb = (POOL_WIDTH + qw) // kvw
    return pl.pallas_call(
        _ctx_att_kernel,
        grid=(BATCH,),
        in_specs=[
            pl.BlockSpec(memory_space=pltpu.SMEM),
            pl.BlockSpec((SEQ, qw), lambda b: (b, POOL_WIDTH // qw)),
            pl.BlockSpec((SEQ, kvw), lambda b: (b, kb)),
            pl.BlockSpec((SEQ, kvw), lambda b: (b, kb + 1)),
        ],
        out_specs=[pl.BlockSpec((SEQ, qw), lambda b: (b, 0)),
                   pl.BlockSpec((None, SEQ, ATT_KV, ATT_HD), lambda b: (b, 0, 0, 0)),
                   pl.BlockSpec((None, SEQ, ATT_KV, ATT_HD), lambda b: (b, 0, 0, 0))],
        out_shape=[jax.ShapeDtypeStruct((N_CTX_TOK, qw), BF16),
                   jax.ShapeDtypeStruct((BATCH, SEQ, ATT_KV, ATT_HD), F32),
                   jax.ShapeDtypeStruct((BATCH, SEQ, ATT_KV, ATT_HD), F32)],
        compiler_params=_cparams(("parallel",)),
        name="attention_ctx",
    )(sink, proj, proj, proj)


def _rope(x, cos, sin):
    lane = lax.broadcasted_iota(jnp.int32, x.shape, 1)
    partner = jnp.where((lane & 63) < 32, pltpu.roll(x, 96, axis=1), pltpu.roll(x, 32, axis=1))
    return x * cos + partner * sin


def _dec_att_kernel(sink_ref, q_ref, k_ref, v_ref, kc_ref, vc_ref, cos_ref, sin_ref, o_ref, kr_ref):
    kv = pl.program_id(1)
    L, B = DEC_SEQ, ATT_BLOCK
    nb = L // B
    kr_ref[...] = _rope(k_ref[...], cos_ref[...], sin_ref[...]).astype(BF16)
    kc = kc_ref[...].astype(BF16)
    vc = vc_ref[...].astype(BF16)
    for i in range(nb):
        lo, hi = max(i - 1, 0) * B, min(i + 2, nb) * B
        rows = slice(i * B, (i + 1) * B)
        kwin = kr_ref[lo:hi, :]
        vals = jnp.concatenate([v_ref[lo:hi, :].astype(BF16), vc], axis=0)
        qpos = i * B + lax.broadcasted_iota(jnp.int32, (B, hi - lo), 0)
        kpos = lo + lax.broadcasted_iota(jnp.int32, (B, hi - lo), 1)
        band = jnp.abs(kpos - qpos) <= ATT_WIN
        for g in range(ATT_GROUP):
            cols = slice(g * ATT_HD, (g + 1) * ATT_HD)
            q = _rope(q_ref[rows, cols], cos_ref[rows, :], sin_ref[rows, :]).astype(BF16)
            s = jnp.concatenate([jnp.where(band, _qk(q, kwin), NEG), _qk(q, kc)], axis=1)
            o_ref[rows, cols] = _softmax_pv(s, vals, sink_ref[kv * ATT_GROUP + g]).astype(BF16)


def _rope_tables():
    half = ATT_HD // 2
    nfreq = half // 2
    freqs = ROPE_BASE ** (-jnp.arange(nfreq, dtype=F32) / nfreq)
    t = jnp.arange(DEC_SEQ)
    pos = jnp.stack([t // GRID_W, t % GRID_W], axis=1).astype(F32)
    ang = pos[:, :, None] * freqs[None, None, :]
    cos = jnp.repeat(jnp.cos(ang), 2, axis=1).reshape(DEC_SEQ, ATT_HD)
    sin = jnp.sin(ang)
    sin = jnp.stack([-sin, sin], axis=2).reshape(DEC_SEQ, ATT_HD)
    return cos, sin


def _dec_attention(proj, cache_k, cache_v, sink, cos, sin, o):
    qw = ATT_GROUP * ATT_HD
    qb = POOL_WIDTH // qw
    kb = (POOL_WIDTH + ATT_HEADS * ATT_HD) // ATT_HD
    blk0 = N_CTX_TOK // DEC_SEQ
    ck = cache_k.reshape(DEC_BATCH, N_ODD, PAST_LEN, ATT_KV * ATT_HD)
    cv = cache_v.reshape(DEC_BATCH, N_ODD, PAST_LEN, ATT_KV * ATT_HD)
    cache_spec = pl.BlockSpec((None, None, PAST_LEN, ATT_HD), lambda b, kv: (b, o, 0, kv))
    table_spec = pl.BlockSpec((DEC_SEQ, ATT_HD), lambda b, kv: (0, 0))
    return pl.pallas_call(
        _dec_att_kernel,
        grid=(DEC_BATCH, ATT_KV),
        in_specs=[
            pl.BlockSpec(memory_space=pltpu.SMEM),
            pl.BlockSpec((DEC_SEQ, qw), lambda b, kv: (blk0 + b, qb + kv)),
            pl.BlockSpec((DEC_SEQ, ATT_HD), lambda b, kv: (blk0 + b, kb + kv)),
            pl.BlockSpec((DEC_SEQ, ATT_HD), lambda b, kv: (blk0 + b, kb + ATT_KV + kv)),
            cache_spec, cache_spec, table_spec, table_spec,
        ],
        out_specs=pl.BlockSpec((DEC_SEQ, qw), lambda b, kv: (b, kv)),
        out_shape=jax.ShapeDtypeStruct((N_DEC_TOK, ATT_HEADS * ATT_HD), BF16),
        scratch_shapes=[pltpu.VMEM((DEC_SEQ, ATT_HD), BF16)],
        compiler_params=_cparams(("parallel", "parallel")),
        name="attention_dec",
    )(sink, proj, proj, proj, ck, cv, cos, sin)


def kernel(x_prompt, x_sample, c, state_s5_re, state_s5_im, state_ret, cache_k, cache_v, c_ctx, w_mod, b_mod, norm_g, ffn1_gate, ffn1_up, ffn1_down, ffn2_gate, ffn2_up, ffn2_down, even_w_in, even_w_out, s5_lam_re, s5_lam_im, s5_log_dt, s5_b_re, s5_b_im, s5_c_re, s5_c_im, s5_d, s5_glu_w, s5_glu_b, ret_decay_logit, ret_gn_g, odd_w_in, odd_w_out, pool_w, pool_scale, att_sink):
    x = jnp.concatenate([x_prompt.reshape(N_CTX_TOK, D_MODEL), x_sample.reshape(N_DEC_TOK, D_MODEL)], axis=0)
    c8 = jnp.concatenate([c_ctx[None, :], c, jnp.zeros((MOD_ROWS - 1 - DEC_BATCH, D_MODEL), F32)], axis=0)
    mod5 = _modulation(c8, w_mod, b_mod)
    norm4 = norm_g.reshape(DEPTH, 6, 1, D_MODEL)
    cos, sin = _rope_tables()
    dec_blk = N_CTX_TOK // DEC_SEQ
    out_re, out_im, out_ret, out_k, out_v = [], [], [], [], []
    for l in range(DEPTH):
        x = _ffn(x, mod5, norm4, ffn1_gate, ffn1_up, ffn1_down, l, 0, 0, 1)
        if l % 2 == 0:
            e = l // 2
            proj = _proj_in(x, mod5, norm4, even_w_in, e, l)
            ops = _s5_operators(s5_lam_re[e], s5_lam_im[e], s5_log_dt[e], s5_b_re[e], s5_b_im[e],
                                s5_c_re[e], s5_c_im[e])
            h0 = jnp.stack([state_s5_re[:, e], state_s5_im[:, e]], axis=2)
            h0 = jnp.transpose(h0, (3, 1, 2, 4, 0)).reshape(S5_GROUPS, 4 * S5_P, DEC_BATCH)
            h0 = jnp.repeat(h0, S5_DEC_CHUNKS, axis=2)
            y_a, s_re, s_im = _s5_mixer(proj, ops, h0, s5_d[e], s5_glu_w, s5_glu_b[e], e)
            lg = jax.nn.log_sigmoid(ret_decay_logit[e].astype(F32))
            yb_ctx, s_ret = _retention(proj, lg, ret_gn_g[e], None, BATCH, SEQ, 0, RET_HEADS)
            (yb_dec,) = _retention(proj, lg, ret_gn_g[e], state_ret[:, e], DEC_BATCH, DEC_SEQ, dec_blk, 2)
            y_b = (yb_ctx, yb_dec)
            out_re.append(s_re)
            out_im.append(s_im)
            out_ret.append(s_ret)
            w_out, w_idx = even_w_out, e
        else:
            o = l // 2
            proj = _proj_in(x, mod5, norm4, odd_w_in, o, l)
            y_a = (_pool(proj, pool_w, o, pool_scale[o], BATCH, SEQ, 0),
                   _pool(proj, pool_w, o, pool_scale[o], DEC_BATCH, DEC_SEQ, dec_blk))
            sink = att_sink[o].astype(F32)
            yd_ctx, k_ctx, v_ctx = _ctx_attention(proj, sink)
            y_b = (yd_ctx, _dec_attention(proj, cache_k, cache_v, sink, cos, sin, o))
            out_k.append(k_ctx)
            out_v.append(v_ctx)
            w_out, w_idx = odd_w_out, o
        x = _proj_out(x, y_a, y_b, mod5, norm4, w_out, w_idx, l)
        if l < DEPTH - 1:
            x = _ffn(x, mod5, norm4, ffn2_gate, ffn2_up, ffn2_down, l, 2, 4, 5)
    ffn2 = functools.partial(_ffn, x, mod5, norm4, ffn2_gate, ffn2_up, ffn2_down, DEPTH - 1, 2, 4, 5)
    y_prompt = ffn2(rows=(0, N_CTX_TOK)).reshape(BATCH, SEQ, D_MODEL)
    y_sample = ffn2(rows=(N_CTX_TOK, N_TOK)).reshape(DEC_BATCH, DEC_SEQ, D_MODEL)
    return (y_prompt, y_sample, jnp.stack(out_re, axis=1), jnp.stack(out_im, axis=1),
            jnp.stack(out_ret, axis=1), jnp.stack(out_k, axis=1), jnp.stack(out_v, axis=1))
```

```python
import functools
import math

import jax
import jax.numpy as jnp
from jax import lax
from jax.experimental import pallas as pl
from jax.experimental.pallas import tpu as pltpu

F32 = jnp.float32
BF16 = jnp.bfloat16

D_MODEL = 2048
BATCH = 16
SEQ = 256
DEPTH = 4
DEC_BATCH = 2
DEC_SEQ = 1024
PAST_LEN = 256
GRID_W = 64
N_EVEN = (DEPTH + 1) // 2
N_ODD = DEPTH // 2
N_MOD = 9
FFN_HIDDEN = 5632
EPS = 1e-6
NEG = -1e30

S5_WIDTH = D_MODEL // 2
S5_GROUP = 16
S5_GROUPS = S5_WIDTH // S5_GROUP
S5_P = 64
RET_HEADS = 8
RET_DK = (D_MODEL // 2) // RET_HEADS
RET_DV = RET_DK
RET_WIDTH = RET_HEADS * RET_DV
RET_CHUNK = 128
EVEN_IN = S5_WIDTH + 4 * RET_WIDTH

POOL_WIDTH = D_MODEL // 2
POOL_WINDOWS = (2, 4, 8, 16)
POOL_GROUPS = 4
POOL_GC = POOL_WIDTH // POOL_GROUPS
ATT_HEADS = 8
ATT_KV = 2
ATT_HD = (D_MODEL // 2) // ATT_HEADS
ATT_GROUP = ATT_HEADS // ATT_KV
ATT_WIN = 128
ATT_BLOCK = 128
ODD_IN = POOL_WIDTH + (ATT_HEADS + 2 * ATT_KV) * ATT_HD
ROPE_BASE = 10000.0

N_CTX_TOK = BATCH * SEQ
N_DEC_TOK = DEC_BATCH * DEC_SEQ
N_TOK = N_CTX_TOK + N_DEC_TOK
MOD_ROWS = 8

S5_T = 16
S5_LANES = S5_T * S5_GROUP
S5_CTX_CHUNKS = SEQ // S5_T
S5_DEC_CHUNKS = DEC_SEQ // S5_T
S5_CTX_ROWS = BATCH * S5_CTX_CHUNKS
S5_DEC_ROWS = DEC_BATCH * S5_DEC_CHUNKS
S5_ROWS = S5_CTX_ROWS + S5_DEC_ROWS

VMEM_LIMIT = 60 * 1024 * 1024


def _cparams(sem):
    return pltpu.CompilerParams(dimension_semantics=sem, vmem_limit_bytes=VMEM_LIMIT)


def _dot(a, b):
    return jnp.dot(a, b, preferred_element_type=F32)


def _unit_rms(x):
    return x * lax.rsqrt(jnp.mean(x * x, axis=-1, keepdims=True) + EPS)


def _mod_row(i, tm):
    start = i * tm
    return (start >= N_CTX_TOK).astype(jnp.int32) + (start >= N_CTX_TOK + DEC_SEQ).astype(jnp.int32)


STACK_TM = 1024


def _stack_kernel(a_ref, b_ref, o_ref):
    is_ctx = pl.program_id(0) < N_CTX_TOK // STACK_TM

    @pl.when(is_ctx)
    def _():
        o_ref[...] = a_ref[...]

    @pl.when(jnp.logical_not(is_ctx))
    def _():
        o_ref[...] = b_ref[...]


def _stack_tokens(x_ctx, x_dec):
    tm = STACK_TM
    nct = N_CTX_TOK // tm
    return pl.pallas_call(
        _stack_kernel,
        grid=(N_TOK // tm,),
        in_specs=[pl.BlockSpec((tm, D_MODEL), lambda i: (jnp.minimum(i, nct - 1), 0)),
                  pl.BlockSpec((tm, D_MODEL), lambda i: (jnp.maximum(i - nct, 0), 0))],
        out_specs=pl.BlockSpec((tm, D_MODEL), lambda i: (i, 0)),
        out_shape=jax.ShapeDtypeStruct((N_TOK, D_MODEL), F32),
        compiler_params=_cparams(("parallel",)),
        name="stack_tokens",
    )(x_ctx, x_dec)


def _mod_kernel(c_ref, w_ref, b_ref, o_ref):
    c = c_ref[...]
    s = (c * jax.nn.sigmoid(c)).astype(BF16)
    o_ref[:, 0, :] = _dot(s, w_ref[...].astype(BF16)) + b_ref[...]


def _modulation(c8, w_mod, b_mod):
    n = N_MOD * D_MODEL
    return pl.pallas_call(
        _mod_kernel,
        grid=(DEPTH, N_MOD),
        in_specs=[
            pl.BlockSpec((MOD_ROWS, D_MODEL), lambda l, j: (0, 0)),
            pl.BlockSpec((None, D_MODEL, D_MODEL), lambda l, j: (l, 0, j)),
            pl.BlockSpec((None, 1, D_MODEL), lambda l, j: (l, 0, j)),
        ],
        out_specs=pl.BlockSpec((None, MOD_ROWS, None, 1, D_MODEL), lambda l, j: (l, 0, j, 0, 0)),
        out_shape=jax.ShapeDtypeStruct((DEPTH, MOD_ROWS, N_MOD, 1, D_MODEL), F32),
        compiler_params=_cparams(("arbitrary", "arbitrary")),
        name="modulation",
    )(c8, w_mod, b_mod.reshape(DEPTH, 1, n))


def _mod_spec(l, j, tm, tile0=0):
    return pl.BlockSpec((None, None, None, 1, D_MODEL),
                        lambda i, *_: (l, _mod_row(tile0 + i, tm), j, 0, 0))


def _gain_spec(l, j):
    return pl.BlockSpec((None, None, 1, D_MODEL), lambda i, *_: (l, j, 0, 0))


FFN_TM = 1024
FFN_TF = 256
SLAB_ROWS = 256


def _adaln_in(x, g_ref, sc_ref, sh_ref):
    return (_unit_rms(x) * (g_ref[...] * (1.0 + sc_ref[...])) + sh_ref[...]).astype(BF16)


def _ffn_kernel(x_ref, sh_ref, sc_ref, gate_ref, gin_ref, gout_ref, wg_ref, wu_ref, wd_ref,
                o_ref, h_ref, *, nf, tm):
    f = pl.program_id(1)
    slabs = [slice(r0, r0 + SLAB_ROWS) for r0 in range(0, tm, SLAB_ROWS)]

    def weights():
        return wg_ref[...].astype(BF16), wu_ref[...].astype(BF16), wd_ref[...].astype(BF16)

    def swiglu(h, w):
        g = _dot(h, w[0])
        u = _dot(h, w[1])
        a = ((g * jax.nn.sigmoid(g)) * u).astype(BF16)
        return _dot(a, w[2])

    @pl.when(f == 0)
    def _():
        w = weights()
        for rows in slabs:
            h = _adaln_in(x_ref[rows, :], gin_ref, sc_ref, sh_ref)
            h_ref[rows, :] = h
            o_ref[rows, :] = swiglu(h, w)

    @pl.when(jnp.logical_and(f > 0, f < nf - 1))
    def _():
        o_ref[...] += swiglu(h_ref[...], weights())

    @pl.when(f == nf - 1)
    def _():
        w = weights()
        for rows in slabs:
            y = o_ref[rows, :] + swiglu(h_ref[rows, :], w)
            o_ref[rows, :] = x_ref[rows, :] + _unit_rms(y) * ((0.5 * gate_ref[...]) * gout_ref[...])


def _ffn(x, mod5, norm4, w_gate, w_up, w_down, l, j, g_in, g_out, rows=(0, N_TOK)):
    tm, tf = FFN_TM, FFN_TF
    nf = FFN_HIDDEN // tf
    tile0, n_tiles = rows[0] // tm, (rows[1] - rows[0]) // tm
    return pl.pallas_call(
        functools.partial(_ffn_kernel, nf=nf, tm=tm),
        grid=(n_tiles, nf),
        in_specs=[
            pl.BlockSpec((tm, D_MODEL), lambda i, f: (tile0 + i, 0)),
            _mod_spec(l, 3 * j, tm, tile0), _mod_spec(l, 3 * j + 1, tm, tile0), _mod_spec(l, 3 * j + 2, tm, tile0),
            _gain_spec(l, g_in), _gain_spec(l, g_out),
            pl.BlockSpec((None, D_MODEL, tf), lambda i, f: (l, 0, f)),
            pl.BlockSpec((None, D_MODEL, tf), lambda i, f: (l, 0, f)),
            pl.BlockSpec((None, tf, D_MODEL), lambda i, f: (l, f, 0)),
        ],
        out_specs=pl.BlockSpec((tm, D_MODEL), lambda i, f: (i, 0)),
        out_shape=jax.ShapeDtypeStruct((n_tiles * tm, D_MODEL), F32),
        scratch_shapes=[pltpu.VMEM((tm, D_MODEL), BF16)],
        compiler_params=_cparams(("parallel", "arbitrary")),
        name="ffn",
    )(x, mod5, mod5, mod5, norm4, norm4, w_gate, w_up, w_down)


PROJ_TM = 1024
PROJ_TN = 1280


def _proj_in_kernel(x_ref, sh_ref, sc_ref, gin_ref, w_ref, o_ref, h_ref, *, tm):
    k = pl.program_id(1)

    @pl.when(k == 0)
    def _():
        w = w_ref[...].astype(BF16)
        for r0 in range(0, tm, SLAB_ROWS):
            rows = slice(r0, r0 + SLAB_ROWS)
            h = _adaln_in(x_ref[rows, :], gin_ref, sc_ref, sh_ref)
            h_ref[rows, :] = h
            o_ref[rows, :] = _dot(h, w)

    @pl.when(k > 0)
    def _():
        o_ref[...] = _dot(h_ref[...], w_ref[...].astype(BF16))


def _proj_in(x, mod5, norm4, w_in, e, l):
    tm, tn = PROJ_TM, PROJ_TN
    n = w_in.shape[-1]
    return pl.pallas_call(
        functools.partial(_proj_in_kernel, tm=tm),
        grid=(N_TOK // tm, n // tn),
        in_specs=[
            pl.BlockSpec((tm, D_MODEL), lambda i, k: (i, 0)),
            _mod_spec(l, 3, tm), _mod_spec(l, 4, tm),
            _gain_spec(l, 2),
            pl.BlockSpec((None, D_MODEL, tn), lambda i, k: (e, 0, k)),
        ],
        out_specs=pl.BlockSpec((tm, tn), lambda i, k: (i, k)),
        out_shape=jax.ShapeDtypeStruct((N_TOK, n), F32),
        scratch_shapes=[pltpu.VMEM((tm, D_MODEL), BF16)],
        compiler_params=_cparams(("parallel", "arbitrary")),
        name="proj_in",
    )(x, mod5, mod5, norm4, w_in)


OUT_TM = 512


def _proj_out_kernel(x_ref, yac_ref, yad_ref, ybc_ref, ybd_ref, gate_ref, gout_ref, w_ref, o_ref, *, tm):
    def run(ya_ref, yb_ref):
        w = w_ref[...].astype(BF16)
        for r0 in range(0, tm, SLAB_ROWS):
            rows = slice(r0, r0 + SLAB_ROWS)
            y = _dot(jnp.concatenate([ya_ref[rows, :], yb_ref[rows, :]], axis=1), w)
            o_ref[rows, :] = x_ref[rows, :] + _unit_rms(y) * (gate_ref[...] * gout_ref[...])

    is_ctx = pl.program_id(0) < N_CTX_TOK // tm
    pl.when(is_ctx)(lambda: run(yac_ref, ybc_ref))
    pl.when(jnp.logical_not(is_ctx))(lambda: run(yad_ref, ybd_ref))


def _proj_out(x, ya, yb, mod5, norm4, w_out, e, l):
    tm = OUT_TM
    half = D_MODEL // 2
    nct = N_CTX_TOK // tm

    def split_specs(y):
        ctx_spec = pl.BlockSpec((tm, half), lambda i: (jnp.minimum(i, nct - 1), 0))
        if isinstance(y, tuple):
            return y, (ctx_spec, pl.BlockSpec((tm, half), lambda i: (jnp.maximum(i - nct, 0), 0)))
        return (y, y), (ctx_spec, pl.BlockSpec((tm, half), lambda i: (jnp.maximum(i, nct), 0)))

    (ya_c, ya_d), (sa_c, sa_d) = split_specs(ya)
    (yb_c, yb_d), (sb_c, sb_d) = split_specs(yb)
    return pl.pallas_call(
        functools.partial(_proj_out_kernel, tm=tm),
        grid=(N_TOK // tm,),
        in_specs=[
            pl.BlockSpec((tm, D_MODEL), lambda i: (i, 0)),
            sa_c, sa_d, sb_c, sb_d,
            _mod_spec(l, 5, tm),
            _gain_spec(l, 3),
            pl.BlockSpec((None, D_MODEL, D_MODEL), lambda i: (e, 0, 0), pipeline_mode=pl.Buffered(1)),
        ],
        out_specs=pl.BlockSpec((tm, D_MODEL), lambda i: (i, 0)),
        out_shape=jax.ShapeDtypeStruct((N_TOK, D_MODEL), F32),
        compiler_params=_cparams(("parallel",)),
        name="proj_out",
    )(x, ya_c, ya_d, yb_c, yb_d, mod5, norm4, w_out)


def _s5_operators(lam_re, lam_im, log_dt, b_re, b_im, c_re, c_im):
    P, G, K = S5_P, S5_GROUPS, S5_GROUP
    flat = lambda z: z.astype(F32).reshape(2, G * P)
    k_major = lambda z, perm: jnp.transpose(z.astype(F32), perm).reshape(2, K, G * P)
    log_dt_rows = jnp.broadcast_to(log_dt.astype(F32)[..., None], (2, G, P)).reshape(2, G * P)
    return pl.pallas_call(
        _s5_factor_kernel,
        out_shape=jax.ShapeDtypeStruct((4, 128, G * P), F32),
        compiler_params=pltpu.CompilerParams(vmem_limit_bytes=VMEM_LIMIT),
        name="s5_factors",
    )(flat(lam_re), flat(lam_im), log_dt_rows,
      k_major(b_re, (0, 3, 1, 2)), k_major(b_im, (0, 3, 1, 2)),
      k_major(c_re, (0, 2, 1, 3)), k_major(c_im, (0, 2, 1, 3)))


S5_F_INC, S5_F_KIN, S5_F_OUT, S5_F_A, S5_F_B, S5_F_C = 0, S5_T, 2 * S5_T, 3 * S5_T, 4 * S5_T, 5 * S5_T


def _s5_factor_kernel(lam_re_ref, lam_im_ref, log_dt_ref, b_re_ref, b_im_ref, c_re_ref, c_im_ref, o_ref):
    T, K = S5_T, S5_GROUP
    o_ref[...] = jnp.zeros_like(o_ref)
    for d in range(2):
        re_part, im_part = 2 * d, 2 * d + 1

        def put(row, zr, zi):
            o_ref[re_part, row:row + zr.shape[0], :] = zr
            o_ref[im_part, row:row + zi.shape[0], :] = zi

        lr = jnp.minimum(lam_re_ref[d:d + 1, :], -1e-4)
        li = lam_im_ref[d:d + 1, :]
        dt = jnp.exp(log_dt_ref[d:d + 1, :])
        mag = jnp.exp(lr * dt)
        ar, ai = mag * jnp.cos(li * dt), mag * jnp.sin(li * dt)
        den = lr * lr + li * li
        kr = ((ar - 1.0) * lr + ai * li) / den
        ki = (ai * lr - (ar - 1.0) * li) / den
        br, bi = b_re_ref[d], b_im_ref[d]
        put(S5_F_B, kr * br - ki * bi, kr * bi + ki * br)
        put(S5_F_C, c_re_ref[d], c_im_ref[d])
        pw = [(jnp.ones_like(ar), jnp.zeros_like(ar))]
        for _ in range(T):
            pr, pi = pw[-1]
            pw.append((pr * ar - pi * ai, pr * ai + pi * ar))
        inv = []
        for pr, pi in pw:
            m = pr * pr + pi * pi
            inv.append((pr / m, -pi / m))
        for s in range(T):
            n_inc, n_kin, n_out = (T - 1 - s, 1 + s, s + 1) if d == 0 else (s, T - s, T - s)
            put(S5_F_INC + s, *pw[n_inc])
            put(S5_F_KIN + s, *inv[n_kin])
            put(S5_F_OUT + s, *pw[n_out])
        put(S5_F_A, *pw[T])


def _cmul_rows(x, ar, ai):
    h = x.shape[0] // 2
    xr, xi = x[:h], x[h:]
    return jnp.concatenate([xr * ar - xi * ai, xr * ai + xi * ar], axis=0)


def _s5_lane_scan(inc, a_rows, h0, lc):
    P = S5_P
    cidx = lax.broadcasted_iota(jnp.int32, (2 * P, 128), 1) & (lc - 1)
    fwd, bwd = inc[:2 * P], inc[2 * P:]
    arf, aif, arb, aib = a_rows[:P], a_rows[P:2 * P], a_rows[2 * P:3 * P], a_rows[3 * P:]
    if h0 is not None:
        fwd = fwd + jnp.where(cidx == 0, _cmul_rows(h0[:2 * P], arf, aif), 0.0)
        bwd = bwd + jnp.where(cidx == lc - 1, _cmul_rows(h0[2 * P:], arb, aib), 0.0)
    d = 1
    while d < lc:
        sh = jnp.where(cidx >= d, pltpu.roll(fwd, d, axis=1), 0.0)
        fwd = fwd + _cmul_rows(sh, arf, aif)
        sh = jnp.where(cidx < lc - d, pltpu.roll(bwd, 128 - d, axis=1), 0.0)
        bwd = bwd + _cmul_rows(sh, arb, aib)
        d *= 2
        if d < lc:
            arf, aif = arf * arf - aif * aif, 2.0 * arf * aif
            arb, aib = arb * arb - aib * aib, 2.0 * arb * aib
    first_f = 0.0 if h0 is None else h0[:2 * P]
    first_b = 0.0 if h0 is None else h0[2 * P:]
    start_f = jnp.where(cidx >= 1, pltpu.roll(fwd, 1, axis=1), first_f)
    start_b = jnp.where(cidx < lc - 1, pltpu.roll(bwd, 127, axis=1), first_b)
    return jnp.concatenate([start_f, start_b], axis=0), jnp.concatenate([fwd, bwd], axis=0)


def _dot_split(a, b):
    a_hi = a.astype(BF16)
    b_hi = b.astype(BF16)
    a_lo = (a - a_hi.astype(F32)).astype(BF16)
    b_lo = (b - b_hi.astype(F32)).astype(BF16)
    return _dot(a_hi, b_hi) + (_dot(a_hi, b_lo) + _dot(a_lo, b_hi))


S5_SLAB = 128 // S5_GROUP


def _split3(x):
    hi = x.astype(BF16)
    r = x - hi.astype(F32)
    mid = r.astype(BF16)
    return hi, mid, (r - mid.astype(F32)).astype(BF16)


def _cprod_rows(a, b, conj_sign):
    h = a.shape[0] // 2
    ar, ai, br, bi = a[:h], a[h:], b[:h], b[h:]
    return jnp.concatenate([ar * br - ai * bi, conj_sign * (ar * bi + ai * br)], axis=0)


def _s5_kernel(u_ref, f_ref, h0_ref, y_ref, fin_ref, xt_ref, yt_ref, fs_ref):
    T, K, P, R = S5_T, S5_GROUP, S5_P, S5_ROWS
    F = S5_LANES
    for gg in range(S5_SLAB):
        fs_ref[gg] = jnp.concatenate([f_ref[part, :, gg * P:(gg + 1) * P] for part in range(4)], axis=1)
    for s in range(T):
        ut = u_ref[pl.ds(s, R, stride=T), :].T
        for gg in range(S5_SLAB):
            xt_ref[gg, s * K:(s + 1) * K, :] = ut[gg * K:(gg + 1) * K, :].astype(BF16)
    shift = K.bit_length() - 1
    tok_out = lax.broadcasted_iota(jnp.int32, (F, F), 0) >> shift
    tok_in = lax.broadcasted_iota(jnp.int32, (F, F), 1) >> shift
    lane = lax.broadcasted_iota(jnp.int32, (128, F), 0)
    feat = lax.broadcasted_iota(jnp.int32, (128, F), 1)
    tok, ch = feat >> shift, feat & (K - 1)

    def spread(field, index):
        return jnp.where(lane == field + index, 1.0, 0.0).astype(BF16)

    e_inc, e_kin, e_out = spread(S5_F_INC, tok), spread(S5_F_KIN, tok), spread(S5_F_OUT, tok)
    e_b, e_c = spread(S5_F_B, ch), spread(S5_F_C, ch)
    lane_a = lax.broadcasted_iota(jnp.int32, (128, 128), 0)
    e_a = jnp.where(lane_a == S5_F_A, 1.0, 0.0).astype(BF16)
    seq = lax.broadcasted_iota(jnp.int32, (128, 128), 1)
    nseq = 128 // S5_CTX_CHUNKS
    sel_last = [jnp.where((lane_a == (seq - o) * S5_CTX_CHUNKS + S5_CTX_CHUNKS - 1) & (seq >= o) & (seq < o + nseq),
                          1.0, 0.0).astype(BF16) for o in (0, nseq)]
    sel_first = [jnp.where((lane_a == (seq - o) * S5_CTX_CHUNKS) & (seq >= o) & (seq < o + nseq),
                           1.0, 0.0).astype(BF16) for o in (0, nseq)]

    def group(gg, carry):
        hi, mid, lo = _split3(fs_ref[gg].T)

        def expand(e, exact=False):
            out = _dot(hi, e) + _dot(mid, e)
            return out + _dot(lo, e) if exact else out

        def cprod(pw, other, conj_sign):
            return jnp.concatenate([_cprod_rows(pw[:2 * P], other[:2 * P], conj_sign),
                                    _cprod_rows(pw[2 * P:], other[2 * P:], conj_sign)], axis=0)

        bb = expand(e_b)
        inc_op = cprod(expand(e_inc), bb, 1.0)
        kin_op = cprod(expand(e_kin), bb, 1.0)
        out_op = cprod(expand(e_out), expand(e_c), -1.0).T
        a_rows = expand(e_a, exact=True)
        resp_f = _dot_split(out_op[:, :2 * P], kin_op[:2 * P])
        resp_b = _dot_split(out_op[:, 2 * P:], kin_op[2 * P:])
        resp = jnp.where(tok_out >= tok_in, resp_f, 0.0) + jnp.where(tok_out <= tok_in, resp_b, 0.0)
        ops = jnp.concatenate([resp, inc_op], axis=0).astype(BF16)
        z = _dot(ops, xt_ref[gg])
        lanes = S5_CTX_ROWS // 2
        st_a, sc_a = _s5_lane_scan(z[F:, :lanes], a_rows, None, S5_CTX_CHUNKS)
        st_b, sc_b = _s5_lane_scan(z[F:, lanes:2 * lanes], a_rows, None, S5_CTX_CHUNKS)
        st_d, _ = _s5_lane_scan(z[F:, 2 * lanes:], a_rows, h0_ref[gg], S5_DEC_CHUNKS)
        start = jnp.concatenate([st_a, st_b, st_d], axis=1).astype(BF16)
        yt_ref[gg] = z[:F] + _dot(out_op.astype(BF16), start)
        fin_f = sum(_dot(t, sel_last[0]) for t in _split3(sc_a[:2 * P])) \
            + sum(_dot(t, sel_last[1]) for t in _split3(sc_b[:2 * P]))
        fin_b = sum(_dot(t, sel_first[0]) for t in _split3(sc_a[2 * P:])) \
            + sum(_dot(t, sel_first[1]) for t in _split3(sc_b[2 * P:]))
        fin_ref[gg] = jnp.concatenate([fin_f, fin_b], axis=0)
        return carry

    lax.fori_loop(0, S5_SLAB, group, 0)
    for i in range(T):
        yt = jnp.concatenate([yt_ref[gg, i * K:(i + 1) * K, :] for gg in range(S5_SLAB)], axis=0)
        y_ref[pl.ds(i, R, stride=T), :] = yt.T


def _s5_scan(proj, factors, h0):
    G, F, P4 = S5_GROUPS, S5_LANES, 4 * S5_P
    assert S5_CTX_ROWS == 256 and S5_DEC_ROWS == 128 and F == 256 and P4 == 256 and S5_T == S5_GROUP
    op_spec = pl.BlockSpec((S5_SLAB, P4, 128), lambda j: (j, 0, 0))
    factor_spec = pl.BlockSpec((4, 128, S5_SLAB * S5_P), lambda j: (0, 0, j))
    return pl.pallas_call(
        _s5_kernel,
        grid=(G // S5_SLAB,),
        in_specs=[pl.BlockSpec((N_TOK, 128), lambda j: (0, j)), factor_spec, op_spec],
        out_specs=[pl.BlockSpec((N_TOK, 128), lambda j: (0, j)), op_spec],
        out_shape=[
            jax.ShapeDtypeStruct((N_TOK, S5_WIDTH), F32),
            jax.ShapeDtypeStruct((G, P4, 128), F32),
        ],
        scratch_shapes=[pltpu.VMEM((S5_SLAB, F, S5_ROWS), BF16), pltpu.VMEM((S5_SLAB, F, S5_ROWS), F32),
                        pltpu.VMEM((S5_SLAB, 128, P4), F32)],
        compiler_params=_cparams(("parallel",)),
        name="s5_scan",
    )(proj, factors, h0)


S5_POST_TM = 512


def _s5_post_kernel(y_ref, u_ref, d_ref, w_ref, b_ref, o_ref):
    y = y_ref[...] + d_ref[...] * u_ref[...]
    z = 0.5 * y * (1.0 + jnp.tanh(math.sqrt(2.0 / math.pi) * (y + 0.044715 * (y * y * y))))
    gate = jax.nn.sigmoid(_dot(z.astype(BF16), w_ref[...].astype(BF16)) + b_ref[...])
    o_ref[...] = (z * gate).astype(BF16)


def _s5_post(y, proj, d_skip, w_glu, b_glu, e):
    tm = S5_POST_TM
    return pl.pallas_call(
        _s5_post_kernel,
        grid=(N_TOK // tm,),
        in_specs=[
            pl.BlockSpec((tm, S5_WIDTH), lambda i: (i, 0)),
            pl.BlockSpec((tm, S5_WIDTH), lambda i: (i, 0)),
            pl.BlockSpec((1, S5_WIDTH), lambda i: (0, 0)),
            pl.BlockSpec((None, S5_WIDTH, S5_WIDTH), lambda i: (e, 0, 0), pipeline_mode=pl.Buffered(1)),
            pl.BlockSpec((1, S5_WIDTH), lambda i: (0, 0)),
        ],
        out_specs=pl.BlockSpec((tm, S5_WIDTH), lambda i: (i, 0)),
        out_shape=jax.ShapeDtypeStruct((N_TOK, S5_WIDTH), BF16),
        compiler_params=_cparams(("parallel",)),
        name="s5_post",
    )(y, proj, d_skip.reshape(1, S5_WIDTH), w_glu, b_glu.reshape(1, S5_WIDTH))


def _s5_mixer(proj, ops, h0, d_skip, w_glu, b_glu, e):
    y, fin = _s5_scan(proj, ops, h0)
    y_a = _s5_post(y, proj, d_skip, w_glu, b_glu, e)
    fin = fin[:, :, :BATCH].reshape(S5_GROUPS, 2, 2, S5_P, BATCH)
    fin = jnp.transpose(fin, (4, 1, 2, 0, 3))
    return y_a, fin[:, :, 0], fin[:, :, 1]


def _ret_head(lgf, lgb, q_ref, k_ref, v_ref, g_ref, gn_ref, o_ref, s0_f, s0_b, cols, seq_len):
    C = RET_CHUNK
    nc = seq_len // C
    row = lax.broadcasted_iota(jnp.int32, (C, C), 0).astype(F32)
    col = lax.broadcasted_iota(jnp.int32, (C, C), 1).astype(F32)
    rel = row - col
    decay = (jnp.where(rel >= 0, jnp.exp(jnp.maximum(rel, 0.0) * lgf), 0.0)
             + jnp.where(rel <= 0, jnp.exp(jnp.maximum(-rel, 0.0) * lgb), 0.0))
    kdec_f = jnp.exp((C - 1.0 - row) * lgf)
    kdec_b = jnp.exp(row * lgb)
    qdec_f = jnp.exp((row + 1.0) * lgf)
    qdec_b = jnp.exp((C - row) * lgb)
    cd_f = jnp.exp(jnp.full((1, RET_DV), C * lgf, F32))
    cd_b = jnp.exp(jnp.full((1, RET_DV), C * lgb, F32))

    def chunk(ref, c):
        return ref[c * C:(c + 1) * C, cols]

    tdot = functools.partial(lax.dot_general, dimension_numbers=(((0,), (0,)), ((), ())),
                             preferred_element_type=F32)
    outs = []
    kv_f, kv_b = [], []
    for c in range(nc):
        q = chunk(q_ref, c)
        k = chunk(k_ref, c) * (RET_DK ** -0.5)
        v = chunk(v_ref, c).astype(BF16)
        inner = lax.dot_general(q.astype(BF16), k.astype(BF16), (((1,), (1,)), ((), ())),
                                preferred_element_type=F32) * decay
        outs.append(_dot(inner.astype(BF16), v))
        kv_f.append(tdot((k * kdec_f).astype(BF16), v))
        kv_b.append(tdot((k * kdec_b).astype(BF16), v))
    s_f = s0_f
    for c in range(nc):
        outs[c] = outs[c] + _dot((chunk(q_ref, c) * qdec_f).astype(BF16), s_f.astype(BF16))
        s_f = cd_f * s_f + kv_f[c]
    s_b = s0_b
    for c in reversed(range(nc)):
        outs[c] = outs[c] + _dot((chunk(q_ref, c) * qdec_b).astype(BF16), s_b.astype(BF16))
        s_b = cd_b * s_b + kv_b[c]
    for c in range(nc):
        o = outs[c]
        mu = jnp.mean(o, axis=-1, keepdims=True)
        var = jnp.mean(jnp.square(o - mu), axis=-1, keepdims=True)
        o = ((o - mu) * lax.rsqrt(var + EPS)) * gn_ref[:, cols]
        g = chunk(g_ref, c)
        o_ref[c * C:(c + 1) * C, cols] = ((g * jax.nn.sigmoid(g)) * o).astype(BF16)
    return s_f, s_b


def _ret_kernel(lg_ref, q_ref, k_ref, v_ref, g_ref, gn_ref, *rest, seq_len, heads, has_state):
    if has_state:
        s0_ref, o_ref = rest
    else:
        o_ref, sfin_ref = rest
    hg = pl.program_id(1)
    for j in range(heads):
        hd = hg * heads + j
        cols = slice(j * RET_DK, (j + 1) * RET_DK)
        if has_state:
            s0_f, s0_b = s0_ref[0, j], s0_ref[1, j]
        else:
            s0_f = s0_b = jnp.zeros((RET_DK, RET_DV), F32)
        s_f, s_b = _ret_head(lg_ref[0, hd], lg_ref[1, hd], q_ref, k_ref, v_ref, g_ref, gn_ref, o_ref,
                             s0_f, s0_b, cols, seq_len)
        if not has_state:
            sfin_ref[0, j] = s_f
            sfin_ref[1, j] = s_b


def _retention(proj, lg, gn_g, s0, n_seq, seq_len, blk0, heads):
    H = RET_HEADS
    width = heads * RET_DK

    def col_spec(off):
        first = (S5_WIDTH + off * RET_WIDTH) // width
        return pl.BlockSpec((seq_len, width), lambda b, hg: (blk0 + b, first + hg))

    state_spec = pl.BlockSpec((None, 2, heads, RET_DK, RET_DV), lambda b, hg: (b, 0, hg, 0, 0))
    in_specs = [
        pl.BlockSpec(memory_space=pltpu.SMEM),
        col_spec(0), col_spec(1), col_spec(2), col_spec(3),
        pl.BlockSpec((1, width), lambda b, hg: (0, hg)),
    ]
    args = [lg, proj, proj, proj, proj, gn_g.reshape(1, RET_WIDTH)]
    out_specs = [pl.BlockSpec((seq_len, width), lambda b, hg: (b, hg))]
    out_shape = [jax.ShapeDtypeStruct((n_seq * seq_len, RET_WIDTH), BF16)]
    if s0 is not None:
        in_specs.append(state_spec)
        args.append(s0)
    else:
        out_specs.append(state_spec)
        out_shape.append(jax.ShapeDtypeStruct((n_seq, 2, H, RET_DK, RET_DV), F32))
    return pl.pallas_call(
        functools.partial(_ret_kernel, seq_len=seq_len, heads=heads, has_state=s0 is not None),
        grid=(n_seq, H // heads),
        in_specs=in_specs,
        out_specs=out_specs,
        out_shape=out_shape,
        compiler_params=_cparams(("parallel", "parallel")),
        name="retention_dec" if s0 is not None else "retention_ctx",
    )(*args)


def _shift_rows(x, d, ridx):
    n = x.shape[0]
    if d == 0:
        return x
    rolled = pltpu.roll(x, d % n, axis=0)
    valid = (ridx >= d) if d > 0 else (ridx < n + d)
    return jnp.where(valid, rolled, 0.0)


def _pool_kernel(u_ref, w_ref, sc_ref, o_ref, *, seq_len):
    L = seq_len
    ridx = lax.broadcasted_iota(jnp.int32, (L, POOL_GC), 0)
    for gi, w in enumerate(POOL_WINDOWS):
        cols = slice(gi * POOL_GC, (gi + 1) * POOL_GC)
        x = u_ref[:, cols]
        half = w // 2
        back, ahead = x, x
        span = 1
        while span < half:
            back = back + _shift_rows(back, span, ridx)
            ahead = ahead + _shift_rows(ahead, -span, ridx)
            span *= 2
        win = _shift_rows(back, 1, ridx) + ahead
        lo = jnp.maximum(ridx - half, 0)
        hi = jnp.minimum(ridx + half - 1, L - 1)
        mean = win / (hi - lo + 1).astype(F32)
        mixed = _dot((mean - x).astype(BF16), w_ref[gi].astype(BF16))
        o_ref[:, cols] = (mixed * sc_ref[:, cols]).astype(BF16)


def _pool(proj, w_pool, o, pool_scale, n_seq, seq_len, blk0):
    return pl.pallas_call(
        functools.partial(_pool_kernel, seq_len=seq_len),
        grid=(n_seq,),
        in_specs=[
            pl.BlockSpec((seq_len, POOL_WIDTH), lambda b: (blk0 + b, 0)),
            pl.BlockSpec((None, POOL_GROUPS, POOL_GC, POOL_GC), lambda b: (o, 0, 0, 0)),
            pl.BlockSpec((1, POOL_WIDTH), lambda b: (0, 0)),
        ],
        out_specs=pl.BlockSpec((seq_len, POOL_WIDTH), lambda b: (b, 0)),
        out_shape=jax.ShapeDtypeStruct((n_seq * seq_len, POOL_WIDTH), BF16),
        compiler_params=_cparams(("parallel",)),
        name="pool",
    )(proj, w_pool, pool_scale.reshape(1, POOL_WIDTH))


def _softmax_pv(s, v, sink):
    m = jnp.maximum(jnp.max(s, axis=-1, keepdims=True), sink)
    p = jnp.exp(s - m)
    denom = jnp.sum(p, axis=-1, keepdims=True) + jnp.exp(sink - m)
    return _dot(p.astype(BF16), v) / denom


def _qk(q, k):
    return lax.dot_general(q, k, (((1,), (1,)), ((), ())), preferred_element_type=F32) * (ATT_HD ** -0.5)


def _ctx_att_kernel(sink_ref, q_ref, k_ref, v_ref, o_ref, ko_ref, vo_ref):
    for kv in range(ATT_KV):
        kv_cols = slice(kv * ATT_HD, (kv + 1) * ATT_HD)
        ko_ref[:, kv, :] = k_ref[:, kv_cols]
        vo_ref[:, kv, :] = v_ref[:, kv_cols]
        k = k_ref[:, kv_cols].astype(BF16)
        v = v_ref[:, kv_cols].astype(BF16)
        for g in range(ATT_GROUP):
            head = kv * ATT_GROUP + g
            cols = slice(head * ATT_HD, (head + 1) * ATT_HD)
            s = _qk(q_ref[:, cols].astype(BF16), k)
            o_ref[:, cols] = _softmax_pv(s, v, sink_ref[head]).astype(BF16)


def _ctx_attention(proj, sink):
    qw = ATT_HEADS * ATT_HD
    kvw = ATT_KV * ATT_HD
    kb = (POOL_WIDTH + qw) // kvw
    return pl.pallas_call(
        _ctx_att_kernel,
        grid=(BATCH,),
        in_specs=[
            pl.BlockSpec(memory_space=pltpu.SMEM),
            pl.BlockSpec((SEQ, qw), lambda b: (b, POOL_WIDTH // qw)),
            pl.BlockSpec((SEQ, kvw), lambda b: (b, kb)),
            pl.BlockSpec((SEQ, kvw), lambda b: (b, kb + 1)),
        ],
        out_specs=[pl.BlockSpec((SEQ, qw), lambda b: (b, 0)),
                   pl.BlockSpec((None, SEQ, ATT_KV, ATT_HD), lambda b: (b, 0, 0, 0)),
                   pl.BlockSpec((None, SEQ, ATT_KV, ATT_HD), lambda b: (b, 0, 0, 0))],
        out_shape=[jax.ShapeDtypeStruct((N_CTX_TOK, qw), BF16),
                   jax.ShapeDtypeStruct((BATCH, SEQ, ATT_KV, ATT_HD), F32),
                   jax.ShapeDtypeStruct((BATCH, SEQ, ATT_KV, ATT_HD), F32)],
        compiler_params=_cparams(("parallel",)),
        name="attention_ctx",
    )(sink, proj, proj, proj)


def _rope(x, cos, sin):
    lane = lax.broadcasted_iota(jnp.int32, x.shape, 1)
    partner = jnp.where((lane & 63) < 32, pltpu.roll(x, 96, axis=1), pltpu.roll(x, 32, axis=1))
    return x * cos + partner * sin


def _dec_att_kernel(sink_ref, q_ref, k_ref, v_ref, kc_ref, vc_ref, cos_ref, sin_ref, o_ref, kr_ref):
    kv = pl.program_id(1)
    L, B = DEC_SEQ, ATT_BLOCK
    nb = L // B
    kr_ref[...] = _rope(k_ref[...], cos_ref[...], sin_ref[...]).astype(BF16)
    kc = kc_ref[...].astype(BF16)
    vc = vc_ref[...].astype(BF16)
    for i in range(nb):
        lo, hi = max(i - 1, 0) * B, min(i + 2, nb) * B
        rows = slice(i * B, (i + 1) * B)
        kwin = kr_ref[lo:hi, :]
        vals = jnp.concatenate([v_ref[lo:hi, :].astype(BF16), vc], axis=0)
        qpos = i * B + lax.broadcasted_iota(jnp.int32, (B, hi - lo), 0)
        kpos = lo + lax.broadcasted_iota(jnp.int32, (B, hi - lo), 1)
        band = jnp.abs(kpos - qpos) <= ATT_WIN
        for g in range(ATT_GROUP):
            cols = slice(g * ATT_HD, (g + 1) * ATT_HD)
            q = _rope(q_ref[rows, cols], cos_ref[rows, :], sin_ref[rows, :]).astype(BF16)
            s = jnp.concatenate([jnp.where(band, _qk(q, kwin), NEG), _qk(q, kc)], axis=1)
            o_ref[rows, cols] = _softmax_pv(s, vals, sink_ref[kv * ATT_GROUP + g]).astype(BF16)


def _rope_tables():
    half = ATT_HD // 2
    nfreq = half // 2
    freqs = ROPE_BASE ** (-jnp.arange(nfreq, dtype=F32) / nfreq)
    t = jnp.arange(DEC_SEQ)
    pos = jnp.stack([t // GRID_W, t % GRID_W], axis=1).astype(F32)
    ang = pos[:, :, None] * freqs[None, None, :]
    cos = jnp.repeat(jnp.cos(ang), 2, axis=1).reshape(DEC_SEQ, ATT_HD)
    sin = jnp.sin(ang)
    sin = jnp.stack([-sin, sin], axis=2).reshape(DEC_SEQ, ATT_HD)
    return cos, sin


def _dec_attention(proj, cache_k, cache_v, sink, cos, sin, o):
    qw = ATT_GROUP * ATT_HD
    qb = POOL_WIDTH // qw
    kb = (POOL_WIDTH + ATT_HEADS * ATT_HD) // ATT_HD
    blk0 = N_CTX_TOK // DEC_SEQ
    ck = cache_k.reshape(DEC_BATCH, N_ODD, PAST_LEN, ATT_KV * ATT_HD)
    cv = cache_v.reshape(DEC_BATCH, N_ODD, PAST_LEN, ATT_KV * ATT_HD)
    cache_spec = pl.BlockSpec((None, None, PAST_LEN, ATT_HD), lambda b, kv: (b, o, 0, kv))
    table_spec = pl.BlockSpec((DEC_SEQ, ATT_HD), lambda b, kv: (0, 0))
    return pl.pallas_call(
        _dec_att_kernel,
        grid=(DEC_BATCH, ATT_KV),
        in_specs=[
            pl.BlockSpec(memory_space=pltpu.SMEM),
            pl.BlockSpec((DEC_SEQ, qw), lambda b, kv: (blk0 + b, qb + kv)),
            pl.BlockSpec((DEC_SEQ, ATT_HD), lambda b, kv: (blk0 + b, kb + kv)),
            pl.BlockSpec((DEC_SEQ, ATT_HD), lambda b, kv: (blk0 + b, kb + ATT_KV + kv)),
            cache_spec, cache_spec, table_spec, table_spec,
        ],
        out_specs=pl.BlockSpec((DEC_SEQ, qw), lambda b, kv: (b, kv)),
        out_shape=jax.ShapeDtypeStruct((N_DEC_TOK, ATT_HEADS * ATT_HD), BF16),
        scratch_shapes=[pltpu.VMEM((DEC_SEQ, ATT_HD), BF16)],
        compiler_params=_cparams(("parallel", "parallel")),
        name="attention_dec",
    )(sink, proj, proj, proj, ck, cv, cos, sin)


def kernel(x_prompt, x_sample, c, state_s5_re, state_s5_im, state_ret, cache_k, cache_v, c_ctx, w_mod, b_mod, norm_g, ffn1_gate, ffn1_up, ffn1_down, ffn2_gate, ffn2_up, ffn2_down, even_w_in, even_w_out, s5_lam_re, s5_lam_im, s5_log_dt, s5_b_re, s5_b_im, s5_c_re, s5_c_im, s5_d, s5_glu_w, s5_glu_b, ret_decay_logit, ret_gn_g, odd_w_in, odd_w_out, pool_w, pool_scale, att_sink):
    x = _stack_tokens(x_prompt.reshape(N_CTX_TOK, D_MODEL), x_sample.reshape(N_DEC_TOK, D_MODEL))
    c8 = jnp.concatenate([c_ctx[None, :], c, jnp.zeros((MOD_ROWS - 1 - DEC_BATCH, D_MODEL), F32)], axis=0)
    mod5 = _modulation(c8, w_mod, b_mod)
    norm4 = norm_g.reshape(DEPTH, 6, 1, D_MODEL)
    cos, sin = _rope_tables()
    dec_blk = N_CTX_TOK // DEC_SEQ
    out_re, out_im, out_ret, out_k, out_v = [], [], [], [], []
    for l in range(DEPTH):
        x = _ffn(x, mod5, norm4, ffn1_gate, ffn1_up, ffn1_down, l, 0, 0, 1)
        if l % 2 == 0:
            e = l // 2
            proj = _proj_in(x, mod5, norm4, even_w_in, e, l)
            ops = _s5_operators(s5_lam_re[e], s5_lam_im[e], s5_log_dt[e], s5_b_re[e], s5_b_im[e],
                                s5_c_re[e], s5_c_im[e])
            h0 = jnp.stack([state_s5_re[:, e], state_s5_im[:, e]], axis=2)
            h0 = jnp.transpose(h0, (3, 1, 2, 4, 0)).reshape(S5_GROUPS, 4 * S5_P, DEC_BATCH)
            h0 = jnp.repeat(h0, S5_DEC_CHUNKS, axis=2)
            y_a, s_re, s_im = _s5_mixer(proj, ops, h0, s5_d[e], s5_glu_w, s5_glu_b[e], e)
            lg = jax.nn.log_sigmoid(ret_decay_logit[e].astype(F32))
            yb_ctx, s_ret = _retention(proj, lg, ret_gn_g[e], None, BATCH, SEQ, 0, RET_HEADS)
            (yb_dec,) = _retention(proj, lg, ret_gn_g[e], state_ret[:, e], DEC_BATCH, DEC_SEQ, dec_blk, 2)
            y_b = (yb_ctx, yb_dec)
            out_re.append(s_re)
            out_im.append(s_im)
            out_ret.append(s_ret)
            w_out, w_idx = even_w_out, e
        else:
            o = l // 2
            proj = _proj_in(x, mod5, norm4, odd_w_in, o, l)
            y_a = (_pool(proj, pool_w, o, pool_scale[o], BATCH, SEQ, 0),
                   _pool(proj, pool_w, o, pool_scale[o], DEC_BATCH, DEC_SEQ, dec_blk))
            sink = att_sink[o].astype(F32)
            yd_ctx, k_ctx, v_ctx = _ctx_attention(proj, sink)
            y_b = (yd_ctx, _dec_attention(proj, cache_k, cache_v, sink, cos, sin, o))
            out_k.append(k_ctx)
            out_v.append(v_ctx)
            w_out, w_idx = odd_w_out, o
        x = _proj_out(x, y_a, y_b, mod5, norm4, w_out, w_idx, l)
        if l < DEPTH - 1:
            x = _ffn(x, mod5, norm4, ffn2_gate, ffn2_up, ffn2_down, l, 2, 4, 5)
    ffn2 = functools.partial(_ffn, x, mod5, norm4, ffn2_gate, ffn2_up, ffn2_down, DEPTH - 1, 2, 4, 5)
    y_prompt = ffn2(rows=(0, N_CTX_TOK)).reshape(BATCH, SEQ, D_MODEL)
    y_sample = ffn2(rows=(N_CTX_TOK, N_TOK)).reshape(DEC_BATCH, DEC_SEQ, D_MODEL)
    return (y_prompt, y_sample, jnp.stack(out_re, axis=1), jnp.stack(out_im, axis=1),
            jnp.stack(out_ret, axis=1), jnp.stack(out_k, axis=1), jnp.stack(out_v, axis=1))
```
